```python
import jax, jax.numpy as jnp
from jax import lax
import numpy as np

D_MODEL = 1024
BATCH = 4
SEQ = 8192
DEPTH = 2
DEC_BATCH = 32
DEC_SEQ = 4
PAST_LEN = 16384
PAGE_SIZE = 128

N_META = 16
HEAD_DIM = 64
A_HEADS = 8
A_KV_HEADS = 4
A_GROUP = A_HEADS // A_KV_HEADS
IDX_HEADS = 8
IDX_DIM = 64
TOPK = 256
B_HEADS = 8
A_WIDTH = A_HEADS * HEAD_DIM
B_WIDTH = B_HEADS * HEAD_DIM
ROPE_THETA = 500000.0
QBLK = 128
EPS = 1e-6
IN_SPLITS = (A_WIDTH, A_KV_HEADS * HEAD_DIM, A_KV_HEADS * HEAD_DIM, IDX_HEADS * IDX_DIM, IDX_DIM,
             IDX_HEADS, A_WIDTH, B_WIDTH, B_WIDTH, B_WIDTH, B_WIDTH, D_MODEL, D_MODEL)
IN_WIDTH = (2 * A_WIDTH + 2 * A_KV_HEADS * HEAD_DIM + IDX_HEADS * IDX_DIM + IDX_DIM + IDX_HEADS
            + 4 * B_WIDTH + 2 * D_MODEL)

kernel_name = 'meta_dsa_stickbreak_hybrid_step'

F32 = jnp.float32


def rmsnorm(x, g):
    x32 = x.astype(F32)
    y = x32 * lax.rsqrt(jnp.mean(x32 * x32, axis=-1, keepdims=True) + EPS)
    return (y * g.astype(F32)).astype(x.dtype)


def partial_rope(x, pos):
    rot = x.shape[-1] // 4
    half = rot // 2
    inv = ROPE_THETA ** (-jnp.arange(half, dtype=F32) * (2.0 / rot))
    ang = pos.astype(F32)[:, None] * inv[None, :]
    cos = jnp.cos(ang)[None, :, None, :]
    sin = jnp.sin(ang)[None, :, None, :]
    x32 = x.astype(F32)
    x1 = x32[..., :half]
    x2 = x32[..., half:rot]
    out = jnp.concatenate([x1 * cos - x2 * sin, x2 * cos + x1 * sin, x32[..., rot:]], axis=-1)
    return out.astype(x.dtype)


def project(h, pos, w_in):
    b, l = h.shape[0], h.shape[1]
    u = h @ w_in
    cuts = np.cumsum(IN_SPLITS)[:-1].tolist()
    qa, ka, va, iq, ik, iw, ga, qb, kb, vb, gb, ma, mb = jnp.split(u, cuts, axis=-1)
    qa = partial_rope(qa.reshape(b, l, A_HEADS, HEAD_DIM), pos)
    ka = partial_rope(ka.reshape(b, l, A_KV_HEADS, HEAD_DIM), pos)
    va = va.reshape(b, l, A_KV_HEADS, HEAD_DIM)
    iq = partial_rope(iq.reshape(b, l, IDX_HEADS, IDX_DIM), pos)
    ik = partial_rope(ik.reshape(b, l, 1, IDX_DIM), pos)[:, :, 0]
    iw = iw * (IDX_HEADS ** -0.5)
    qb = qb.reshape(b, l, B_HEADS, HEAD_DIM)
    kv_a = jnp.stack([ka, va], axis=2)
    kv_b = jnp.stack([kb.reshape(b, l, B_HEADS, HEAD_DIM), vb.reshape(b, l, B_HEADS, HEAD_DIM)], axis=2)
    return qa, kv_a, iq, ik, iw, ga, qb, kv_b, gb, ma, mb


def indexer_topk(iq, iw, ik, q_pos, k_top):
    s = ik.shape[1]
    dots = jnp.einsum('bqhd,bsd->bqhs', iq.astype(F32), ik.astype(F32)) * (IDX_DIM ** -0.5)
    score = jnp.einsum('bqhs,bqh->bqs', jax.nn.relu(dots), iw.astype(F32))
    visible = jnp.arange(s)[None, :] <= q_pos[:, None]
    score = jnp.where(visible[None], score, -jnp.inf)
    _, sel = lax.top_k(score, k_top)
    valid = sel <= q_pos[None, :, None]
    return sel, valid


def sparse_attend(q, kv_sel, valid):
    b, nq = q.shape[0], q.shape[1]
    qg = q.reshape(b, nq, A_KV_HEADS, A_GROUP, HEAD_DIM).astype(F32)
    logits = jnp.einsum('bqhgd,bqkhd->bqhgk', qg, kv_sel[:, :, :, 0].astype(F32)) * (HEAD_DIM ** -0.5)
    logits = jnp.where(valid[:, :, None, None, :], logits, -jnp.inf)
    p = jax.nn.softmax(logits, axis=-1)
    out = jnp.einsum('bqhgk,bqkhd->bqhgd', p, kv_sel[:, :, :, 1].astype(F32))
    return out.reshape(b, nq, A_WIDTH).astype(q.dtype)


def stick_breaking(q, kv, q_pos):
    s = kv.shape[1]
    z = jnp.einsum('bqhd,bshd->bhqs', q.astype(F32), kv[:, :, 0].astype(F32)) * (HEAD_DIM ** -0.5)
    earlier = (jnp.arange(s)[None, :] < q_pos[:, None])[None, None]
    log_rest = jnp.where(earlier, jax.nn.log_sigmoid(-z), 0.0)
    after = lax.cumsum(log_rest, axis=3, reverse=True) - log_rest
    a = jnp.where(earlier, jnp.exp(jax.nn.log_sigmoid(z) + after), 0.0)
    out = jnp.einsum('bhqs,bshd->bqhd', a, kv[:, :, 1].astype(F32))
    return out.reshape(q.shape[0], q.shape[1], B_WIDTH).astype(q.dtype)


def merge(oa, ga, ob, gb, ma, mb, w_oa, w_ob, w_out):
    pa = (oa * jax.nn.silu(ga)) @ w_oa
    pb = (ob * jax.nn.silu(gb)) @ w_ob
    return (jax.nn.sigmoid(ma) * pa + jax.nn.sigmoid(mb) * pb) @ w_out


def to_blocks(a):
    return jnp.moveaxis(a.reshape((a.shape[0], -1, QBLK) + a.shape[2:]), 1, 0)


def from_blocks(a):
    a = jnp.moveaxis(a, 0, 1)
    return a.reshape((a.shape[0], -1) + a.shape[3:])


def prompt_mixers(qa, kv_a, iq, ik, iw, qb, kv_b, k_top):
    b, lp = qa.shape[0], qa.shape[1]
    bidx = jnp.arange(b)[:, None, None]

    def block(args):
        qa_blk, iq_blk, iw_blk, qb_blk, t0 = args
        q_pos = t0 + jnp.arange(QBLK)
        sel, valid = indexer_topk(iq_blk, iw_blk, ik, q_pos, k_top)
        oa = sparse_attend(qa_blk, kv_a[bidx, sel], valid)
        ob = stick_breaking(qb_blk, kv_b, q_pos)
        return oa, ob

    starts = jnp.arange(lp // QBLK, dtype=jnp.int32) * QBLK
    oa, ob = lax.map(block, (to_blocks(qa), to_blocks(iq), to_blocks(iw), to_blocks(qb), starts))
    return from_blocks(oa), from_blocks(ob)


def setup_inputs(seed: int = 0) -> dict:
    key = jax.random.key(seed)
    ks = jax.random.split(key, 14)
    n_pages = PAST_LEN // PAGE_SIZE
    n_used = DEC_BATCH * n_pages
    n_pool = n_used + max(1, n_used // 4)
    page_table = jax.random.permutation(ks[5], n_pool)[:n_used].reshape(DEC_BATCH, n_pages).astype(jnp.int32)
    return {
        'x_prompt': jax.random.normal(ks[0], (BATCH, SEQ, D_MODEL), F32),
        'x_sample': jax.random.normal(ks[1], (DEC_BATCH, DEC_SEQ, D_MODEL), F32),
        'cache_a_kv': jax.random.normal(ks[2], (DEPTH, n_pool, PAGE_SIZE, 2, A_KV_HEADS, HEAD_DIM), F32),
        'cache_a_idx': jax.random.normal(ks[3], (DEPTH, n_pool, PAGE_SIZE, IDX_DIM), F32),
        'cache_b_kv': jax.random.normal(ks[4], (DEPTH, n_pool, PAGE_SIZE, 2, B_HEADS, HEAD_DIM), F32),
        'page_table': page_table,
        'meta': jax.random.normal(ks[6], (N_META, D_MODEL), F32),
        'norm_g': 1.0 + 0.02 * jax.random.normal(ks[7], (DEPTH, D_MODEL), F32),
        'w_in': jax.random.normal(ks[8], (DEPTH, D_MODEL, IN_WIDTH), F32) * (D_MODEL ** -0.5),
        'w_oa': jax.random.normal(ks[9], (DEPTH, A_WIDTH, D_MODEL), F32) * (A_WIDTH ** -0.5),
        'w_ob': jax.random.normal(ks[10], (DEPTH, B_WIDTH, D_MODEL), F32) * (B_WIDTH ** -0.5),
        'w_out': jax.random.normal(ks[11], (DEPTH, D_MODEL, D_MODEL), F32) * (D_MODEL ** -0.5),
        'final_g': 1.0 + 0.02 * jax.random.normal(ks[12], (D_MODEL,), F32),
    }


def reference(x_prompt, x_sample, cache_a_kv, cache_a_idx, cache_b_kv, page_table,
              meta, norm_g, w_in, w_oa, w_ob, w_out, final_g):
    b, seq = x_prompt.shape[0], x_prompt.shape[1]
    l = N_META + seq
    lp = -(-l // QBLK) * QBLK
    x = jnp.concatenate([jnp.broadcast_to(meta[None].astype(x_prompt.dtype), (b, N_META, D_MODEL)), x_prompt], axis=1)
    x = jnp.pad(x, ((0, 0), (0, lp - l), (0, 0)))
    pos_p = jnp.arange(lp)
    k_top_p = min(TOPK, seq // 4)
    akv_p, aidx_p, bkv_p = [], [], []
    for layer in range(DEPTH):
        h = rmsnorm(x, norm_g[layer])
        qa, kv_a, iq, ik, iw, ga, qb, kv_b, gb, ma, mb = project(h, pos_p, w_in[layer])
        oa, ob = prompt_mixers(qa, kv_a, iq, ik, iw, qb, kv_b, k_top_p)
        x = x + merge(oa, ga, ob, gb, ma, mb, w_oa[layer], w_ob[layer], w_out[layer])
        akv_p.append(kv_a[:, :l])
        aidx_p.append(ik[:, :l])
        bkv_p.append(kv_b[:, :l])
    y_prompt = rmsnorm(x[:, N_META:l], final_g)

    db, ds = x_sample.shape[0], x_sample.shape[1]
    past = page_table.shape[1] * PAGE_SIZE
    pos_s = past + jnp.arange(ds)
    k_top_s = min(TOPK, (past + ds) // 4)
    bidx = jnp.arange(db)[:, None, None]
    xs = x_sample
    akv_s, aidx_s, bkv_s = [], [], []
    for layer in range(DEPTH):
        h = rmsnorm(xs, norm_g[layer])
        qa, kv_a, iq, ik, iw, ga, qb, kv_b, gb, ma, mb = project(h, pos_s, w_in[layer])
        ik_past = cache_a_idx[layer, page_table].reshape(db, past, IDX_DIM).astype(ik.dtype)
        ik_all = jnp.concatenate([ik_past, ik], axis=1)
        sel, valid = indexer_topk(iq, iw, ik_all, pos_s, k_top_s)
        in_past = sel < past
        sel_past = jnp.minimum(sel, past - 1)
        phys_page = page_table[bidx, sel_past // PAGE_SIZE]
        kv_past_sel = cache_a_kv[layer, phys_page, sel_past % PAGE_SIZE].astype(kv_a.dtype)
        kv_new_sel = kv_a[bidx, jnp.clip(sel - past, 0, ds - 1)]
        kv_sel = jnp.where(in_past[..., None, None, None], kv_past_sel, kv_new_sel)
        oa = sparse_attend(qa, kv_sel, valid)
        kv_b_past = cache_b_kv[layer, page_table].reshape(db, past, 2, B_HEADS, HEAD_DIM).astype(kv_b.dtype)
        ob = stick_breaking(qb, jnp.concatenate([kv_b_past, kv_b], axis=1), pos_s)
        xs = xs + merge(oa, ga, ob, gb, ma, mb, w_oa[layer], w_ob[layer], w_out[layer])
        akv_s.append(kv_a)
        aidx_s.append(ik)
        bkv_s.append(kv_b)
    y_sample = rmsnorm(xs, final_g)

    new_a_kv_prompt = jnp.stack(akv_p)
    new_a_idx_prompt = jnp.stack(aidx_p)
    new_b_kv_prompt = jnp.stack(bkv_p)
    new_a_kv_sample = jnp.stack(akv_s)
    new_a_idx_sample = jnp.stack(aidx_s)
    new_b_kv_sample = jnp.stack(bkv_s)
    return (y_prompt, y_sample, new_a_kv_prompt, new_a_idx_prompt, new_b_kv_prompt,
            new_a_kv_sample, new_a_idx_sample, new_b_kv_sample)
```

```python
import functools
import math

import numpy as np
import jax
import jax.numpy as jnp
from jax import lax
from jax.experimental import pallas as pl
from jax.experimental.pallas import tpu as pltpu

F32 = jnp.float32
BF16 = jnp.bfloat16
I32 = jnp.int32

HEAD_DIM = 64
A_HEADS = 8
A_KV_HEADS = 4
A_GROUP = A_HEADS // A_KV_HEADS
IDX_HEADS = 8
IDX_DIM = 64
B_HEADS = 8
TOPK = 256
ROPE_THETA = 500000.0
EPS = 1e-6
A_WIDTH = A_HEADS * HEAD_DIM
A_KV_WIDTH = A_KV_HEADS * HEAD_DIM
B_WIDTH = B_HEADS * HEAD_DIM
IDX_WIDTH = IDX_HEADS * IDX_DIM

LANES = 128
SUBLANES = 8
VMEM_LIMIT = 56 * 1024 * 1024
INT_MIN = -(2 ** 31)
INT_MAX = 2 ** 31 - 1
NEG_BIG = -1e30

TQ = 256
TK = 256
TR = 256


def _nt_dot(a, b):
    return lax.dot_general(a, b, (((1,), (1,)), ((), ())), preferred_element_type=F32)


def _sigmoid(x):
    return 1.0 / (1.0 + jnp.exp(-x))


_OFF_QA = 0
_OFF_KA = _OFF_QA + A_WIDTH
_OFF_VA = _OFF_KA + A_KV_WIDTH
_OFF_IQ = _OFF_VA + A_KV_WIDTH
_OFF_IK = _OFF_IQ + IDX_WIDTH
_OFF_GA = _OFF_IK + LANES
_OFF_QB = _OFF_GA + A_WIDTH
_OFF_KB = _OFF_QB + B_WIDTH
_OFF_VB = _OFF_KB + B_WIDTH
_OFF_GB = _OFF_VB + B_WIDTH


def _pad_w_in(w, d_model):
    cut = _OFF_IK + IDX_DIM + IDX_HEADS
    pad = LANES - IDX_DIM - IDX_HEADS
    return jnp.concatenate([w[:, :cut], jnp.zeros((w.shape[0], pad), w.dtype), w[:, cut:]], axis=1)


def _rope_tables(pos):
    rot = HEAD_DIM // 4
    half = rot // 2
    inv = ROPE_THETA ** (-jnp.arange(half, dtype=F32) * (2.0 / rot))
    ang = pos.astype(F32)[:, None] * inv[None, :]
    cos = jnp.cos(ang)
    sin = jnp.sin(ang)
    n = pos.shape[0]
    one = jnp.ones((n, HEAD_DIM - rot), F32)
    zero = jnp.zeros((n, HEAD_DIM - rot), F32)
    zh = jnp.zeros((n, half), F32)
    c = jnp.concatenate([cos, cos, one], axis=1)
    sa = jnp.concatenate([-sin, zh, zero], axis=1)
    sb = jnp.concatenate([zh, sin, zero], axis=1)
    two = lambda t: jnp.concatenate([t, t], axis=1)
    return two(c), two(sa), two(sb)


def _rope(y, c, sa, sb):
    half = HEAD_DIM // 8
    return y * c + pltpu.roll(y, LANES - half, 1) * sa + pltpu.roll(y, half, 1) * sb


def _proj_kernel(x_ref, g_ref, w_ref, c_ref, sa_ref, sb_ref,
                 qa_ref, kva_ref, kah_ref, vah_ref, iq_ref, ik_ref, ikb_ref, iw_ref,
                 ga_ref, qb_ref, kvb_ref, kbh_ref, vbh_ref, gb_ref, ma_ref, mb_ref, *, d_model):
    x = x_ref[0]
    ms = jnp.mean(x * x, axis=-1, keepdims=True)
    h = (x * lax.rsqrt(ms + EPS) * g_ref[...]).astype(BF16)
    c = c_ref[...]
    sa = sa_ref[...]
    sb = sb_ref[...]

    def seg(off, width):
        return jnp.dot(h, w_ref[:, off:off + width], preferred_element_type=F32)

    def roped(u, groups):
        return [_rope(u[:, i * LANES:(i + 1) * LANES], c, sa, sb) for i in range(groups)]

    def put_heads(ref, groups, scale):
        for i, y in enumerate(groups):
            ys = (y * scale).astype(ref.dtype) if scale != 1.0 else y.astype(ref.dtype)
            ref[0, 2 * i] = ys[:, :HEAD_DIM]
            ref[0, 2 * i + 1] = ys[:, HEAD_DIM:]

    def split(u):
        return [u[:, i * LANES:(i + 1) * LANES] for i in range(u.shape[1] // LANES)]

    qscale = HEAD_DIM ** -0.5
    put_heads(qa_ref, roped(seg(_OFF_QA, A_WIDTH), A_WIDTH // LANES), qscale)
    ka = roped(seg(_OFF_KA, A_KV_WIDTH), A_KV_WIDTH // LANES)
    va = split(seg(_OFF_VA, A_KV_WIDTH))
    for i, y in enumerate(ka):
        kva_ref[0, :, i * LANES:(i + 1) * LANES] = y
    for i, y in enumerate(va):
        kva_ref[0, :, A_KV_WIDTH + i * LANES:A_KV_WIDTH + (i + 1) * LANES] = y
    put_heads(kah_ref, ka, 1.0)
    put_heads(vah_ref, va, 1.0)
    put_heads(iq_ref, roped(seg(_OFF_IQ, IDX_WIDTH), IDX_WIDTH // LANES), IDX_DIM ** -0.5)
    u = seg(_OFF_IK, LANES)
    ik = _rope(u, c, sa, sb)[:, :IDX_DIM]
    ik_ref[0] = ik
    ikb_ref[0] = ik.astype(BF16)
    iw_ref[0] = u[:, IDX_DIM:IDX_DIM + IDX_HEADS] * (IDX_HEADS ** -0.5)
    ga_ref[0] = seg(_OFF_GA, A_WIDTH)
    gb_ref[0] = seg(_OFF_GB, B_WIDTH)
    put_heads(qb_ref, split(seg(_OFF_QB, B_WIDTH)), qscale)
    kb = seg(_OFF_KB, B_WIDTH)
    vb = seg(_OFF_VB, B_WIDTH)
    kvb_ref[0, :, :B_WIDTH] = kb
    kvb_ref[0, :, B_WIDTH:] = vb
    put_heads(kbh_ref, split(kb), 1.0)
    put_heads(vbh_ref, split(vb), 1.0)
    off_ma = _OFF_GB + B_WIDTH
    ma_ref[0] = seg(off_ma, d_model)
    mb_ref[0] = seg(off_ma + d_model, d_model)


def _project(x, g, w_pad, tables, tr):
    G, L, D = x.shape
    c, sa, sb = tables
    wn = w_pad.shape[1]
    f = lambda w: jax.ShapeDtypeStruct((G, L, w), F32)
    hm = lambda nh: jax.ShapeDtypeStruct((G, nh, L, HEAD_DIM), BF16)
    out_shape = (
        hm(A_HEADS), f(2 * A_KV_WIDTH), hm(A_KV_HEADS), hm(A_KV_HEADS),
        hm(IDX_HEADS), f(IDX_DIM), jax.ShapeDtypeStruct((G, L, IDX_DIM), BF16), f(IDX_HEADS),
        f(A_WIDTH), hm(B_HEADS), f(2 * B_WIDTH), hm(B_HEADS), hm(B_HEADS), f(B_WIDTH), f(D), f(D),
    )
    row = lambda w: pl.BlockSpec((1, tr, w), lambda b, i: (b, i, 0))
    hms = lambda nh: pl.BlockSpec((1, nh, tr, HEAD_DIM), lambda b, i: (b, 0, i, 0))
    tab = pl.BlockSpec((tr, LANES), lambda b, i: (i, 0))
    out_specs = (
        hms(A_HEADS), row(2 * A_KV_WIDTH), hms(A_KV_HEADS), hms(A_KV_HEADS),
        hms(IDX_HEADS), row(IDX_DIM), row(IDX_DIM), row(IDX_HEADS),
        row(A_WIDTH), hms(B_HEADS), row(2 * B_WIDTH), hms(B_HEADS), hms(B_HEADS), row(B_WIDTH), row(D), row(D),
    )
    names = ("qa", "kva", "kah", "vah", "iq", "ik", "ikb", "iw",
             "ga", "qb", "kvb", "kbh", "vbh", "gb", "ma", "mb")
    outs = pl.pallas_call(
        functools.partial(_proj_kernel, d_model=D),
        grid=(G, L // tr),
        in_specs=[row(D), pl.BlockSpec((1, D), lambda b, i: (0, 0)),
                  pl.BlockSpec((D, wn), lambda b, i: (0, 0)), tab, tab, tab],
        out_specs=out_specs,
        out_shape=out_shape,
        compiler_params=pltpu.CompilerParams(
            dimension_semantics=("arbitrary", "arbitrary"), vmem_limit_bytes=VMEM_LIMIT),
        name="proj",
    )(x, g.reshape(1, D), w_pad, c, sa, sb)
    return dict(zip(names, outs))


def _score_keys(iq_ref, wb_ref, ik, visible):
    reps = ik.shape[0] // LANES
    s = None
    for h in range(IDX_HEADS):
        w = wb_ref[h]
        if reps > 1:
            w = jnp.concatenate([w] * reps, axis=1)
        t = w * jnp.maximum(_nt_dot(iq_ref[h], ik), 0.0)
        s = t if s is None else s + t
    return _keys_of(s, visible)


def _keys_of(s, visible):
    s = s + 0.0
    bits = pltpu.bitcast(s, I32)
    key = bits ^ ((bits >> 31) & INT_MAX)
    if visible is not None:
        key = jnp.where(visible, key, INT_MIN)
    return key


def _bcast_w(iw, wb_ref):
    for h in range(IDX_HEADS):
        wb_ref[h] = jnp.broadcast_to(iw[:, h:h + 1], (iw.shape[0], LANES))


def _select_threshold(keys_ref, nblk, k_top, tq, tk, max_cols):
    def count(pred):
        def body(j, acc):
            return acc + jnp.where(pred(j, keys_ref[j]), 1, 0)

        acc = lax.fori_loop(0, nblk, body, jnp.zeros((tq, tk), I32))
        return jnp.sum(acc, axis=1, keepdims=True)

    def bit_body(it, prefix):
        bit = lax.shift_left(jnp.int32(1), jnp.int32(31) - it)
        cand = prefix + bit
        cb = jnp.broadcast_to(cand, (tq, tk))
        return jnp.where(count(lambda j, kj: kj >= cb) >= k_top, cand, prefix)

    thr = lax.fori_loop(0, 32, bit_body, jnp.full((tq, 1), INT_MIN, I32))
    tb = jnp.broadcast_to(thr, (tq, tk))
    n_gt = count(lambda j, kj: kj > tb)
    n_eq = count(lambda j, kj: kj == tb)
    need = k_top - n_gt
    few = thr == INT_MIN
    excess = jnp.logical_and(n_eq > need, jnp.logical_not(few))
    nbits = max(1, int(max_cols).bit_length())
    lane = lax.broadcasted_iota(I32, (tq, tk), 1)

    def tie_search(_):
        def bit_body2(it, lo):
            cand = lo + lax.shift_left(jnp.int32(1), jnp.int32(nbits - 1) - it)
            cb = jnp.broadcast_to(cand, (tq, tk))
            n = count(lambda j, kj: jnp.logical_and(kj == tb, j * tk + lane < cb))
            return jnp.where(n < need, cand, lo)

        return lax.fori_loop(0, nbits, bit_body2, jnp.zeros((tq, 1), I32))

    any_excess = jnp.max(jnp.where(excess, 1, 0)) > 0
    cidx = lax.cond(any_excess, tie_search, lambda _: jnp.zeros((tq, 1), I32), 0)
    cidx = jnp.where(excess, cidx, INT_MAX)
    cidx = jnp.where(few, -1, cidx)
    return thr, cidx


def _selected(key, col, thr_b, cidx_b):
    return jnp.logical_or(key > thr_b, jnp.logical_and(key == thr_b, col <= cidx_b))


def _prompt_select_kernel(iq_ref, iw_ref, ik_ref, thr_ref, cidx_ref, keys_ref, wb_ref, *, k_top):
    i = pl.program_id(1)
    _bcast_w(iw_ref[0], wb_ref)
    row = i * TQ + lax.broadcasted_iota(I32, (TQ, TK), 0)
    lane = lax.broadcasted_iota(I32, (TQ, TK), 1)

    def fill(j, _):
        ik = ik_ref[0, pl.ds(pl.multiple_of(j * TK, TK), TK), :]
        keys_ref[j] = _score_keys(iq_ref.at[0], wb_ref, ik, (j * TK + lane) <= row)
        return 0

    nblk = (i * TQ + TQ + TK - 1) // TK
    lax.fori_loop(0, nblk, fill, 0)
    thr, cidx = _select_threshold(keys_ref, nblk, k_top, TQ, TK, keys_ref.shape[0] * TK)
    thr_ref[0] = thr
    cidx_ref[0] = cidx


def _prompt_select(iq, iw, ikb, k_top):
    B, H, L, _ = iq.shape
    nq = L // TQ
    col1 = pl.BlockSpec((1, TQ, 1), lambda b, i: (b, i, 0))
    return pl.pallas_call(
        functools.partial(_prompt_select_kernel, k_top=k_top),
        grid=(B, nq),
        in_specs=[pl.BlockSpec((1, H, TQ, IDX_DIM), lambda b, i: (b, 0, i, 0)),
                  pl.BlockSpec((1, TQ, IDX_HEADS), lambda b, i: (b, i, 0)),
                  pl.BlockSpec((1, L, IDX_DIM), lambda b, i: (b, 0, 0))],
        out_specs=(col1, col1),
        out_shape=(jax.ShapeDtypeStruct((B, L, 1), I32), jax.ShapeDtypeStruct((B, L, 1), I32)),
        scratch_shapes=[pltpu.VMEM((L // TK, TQ, TK), I32), pltpu.VMEM((IDX_HEADS, TQ, LANES), F32)],
        compiler_params=pltpu.CompilerParams(
            dimension_semantics=("arbitrary", "arbitrary"), vmem_limit_bytes=VMEM_LIMIT),
        name="prompt_select",
    )(iq, iw, ikb)


def _tri_tables(nq, descending):
    it, jt = [], []
    for i in range(nq):
        js = range(i, -1, -1) if descending else range(i + 1)
        for j in js:
            it.append(i)
            jt.append(j)
    return jnp.asarray(np.array(it, np.int32)), jnp.asarray(np.array(jt, np.int32))


def _prompt_dsa_kernel(it_ref, jt_ref, iq_ref, iw_ref, ik_ref, thr_ref, cidx_ref, q_ref, k_ref, v_ref,
                       o_ref, wb_ref, m_ref, l_ref, acc_ref):
    p = pl.program_id(1)
    i = it_ref[p]
    j = jt_ref[p]

    @pl.when(j == 0)
    def _():
        _bcast_w(iw_ref[0], wb_ref)
        m_ref[...] = jnp.full(m_ref.shape, NEG_BIG, F32)
        l_ref[...] = jnp.zeros(l_ref.shape, F32)
        acc_ref[...] = jnp.zeros(acc_ref.shape, F32)

    row = i * TQ + lax.broadcasted_iota(I32, (TQ, TK), 0)
    col = j * TK + lax.broadcasted_iota(I32, (TQ, TK), 1)
    key = _score_keys(iq_ref.at[0], wb_ref, ik_ref[0], col <= row)
    sel = _selected(key, col, jnp.broadcast_to(thr_ref[0], (TQ, TK)), jnp.broadcast_to(cidx_ref[0], (TQ, TK)))
    reps = TK // LANES

    def head(h, _):
        g = h // A_GROUP
        s = _nt_dot(q_ref[0, h], k_ref[0, g])
        s = jnp.where(sel, s, 2.0 * NEG_BIG)
        m_prev = m_ref[h]
        m_new = jnp.maximum(m_prev, jnp.max(s, axis=1, keepdims=True))
        alpha = jnp.exp(m_prev - m_new)
        pr = jnp.exp(s - jnp.concatenate([m_new] * reps, axis=1))
        l_ref[h] = alpha * l_ref[h] + jnp.sum(pr, axis=1, keepdims=True)
        acc_ref[h] = alpha[:, :HEAD_DIM] * acc_ref[h] + jnp.dot(
            pr.astype(BF16), v_ref[0, g], preferred_element_type=F32)
        m_ref[h] = m_new
        return 0

    lax.fori_loop(0, A_HEADS, head, 0)

    @pl.when(j == i)
    def _():
        for h in range(A_HEADS):
            o_ref[0, :, h * HEAD_DIM:(h + 1) * HEAD_DIM] = acc_ref[h] / l_ref[h][:, :HEAD_DIM]


def _prompt_dsa(iq, iw, ikb, thr, cidx, qa, kah, vah):
    B, H, L, _ = qa.shape
    nq = L // TQ
    it, jt = _tri_tables(nq, descending=False)
    qspec = lambda nh: pl.BlockSpec((1, nh, TQ, HEAD_DIM), lambda b, p, it, jt: (b, 0, it[p], 0))
    kspec = lambda nh: pl.BlockSpec((1, nh, TK, HEAD_DIM), lambda b, p, it, jt: (b, 0, jt[p], 0))
    qrow = lambda w: pl.BlockSpec((1, TQ, w), lambda b, p, it, jt: (b, it[p], 0))
    grid_spec = pltpu.PrefetchScalarGridSpec(
        num_scalar_prefetch=2,
        grid=(B, int(it.shape[0])),
        in_specs=[qspec(IDX_HEADS), qrow(IDX_HEADS),
                  pl.BlockSpec((1, TK, IDX_DIM), lambda b, p, it, jt: (b, jt[p], 0)),
                  qrow(1), qrow(1), qspec(A_HEADS), kspec(A_KV_HEADS), kspec(A_KV_HEADS)],
        out_specs=qrow(A_WIDTH),
        scratch_shapes=[pltpu.VMEM((IDX_HEADS, TQ, LANES), F32),
                        pltpu.VMEM((A_HEADS, TQ, LANES), F32),
                        pltpu.VMEM((A_HEADS, TQ, LANES), F32),
                        pltpu.VMEM((A_HEADS, TQ, HEAD_DIM), F32)],
    )
    return pl.pallas_call(
        _prompt_dsa_kernel,
        grid_spec=grid_spec,
        out_shape=jax.ShapeDtypeStruct((B, L, A_WIDTH), F32),
        compiler_params=pltpu.CompilerParams(
            dimension_semantics=("arbitrary", "arbitrary"), vmem_limit_bytes=VMEM_LIMIT),
        name="prompt_dsa",
    )(it, jt, iq, iw, ikb, thr, cidx, qa, kah, vah)


def _suffix_matrix(n):
    r = lax.broadcasted_iota(I32, (n, n), 0)
    c = lax.broadcasted_iota(I32, (n, n), 1)
    return jnp.where(r >= c, 1.0, 0.0).astype(BF16)


def _stick_tile(z, earlier, r_b, suf):
    e = jnp.exp(-jnp.abs(z))
    l1p = jnp.log(1.0 + e)
    lneg = -(jnp.maximum(z, 0.0) + l1p)
    lpos = jnp.minimum(z, 0.0) - l1p
    if earlier is not None:
        lneg = jnp.where(earlier, lneg, 0.0)
    hi = lneg.astype(BF16)
    lo = (lneg - hi.astype(F32)).astype(BF16)
    incl = jnp.dot(hi, suf, preferred_element_type=F32) + jnp.dot(lo, suf, preferred_element_type=F32)
    reps = z.shape[1] // LANES
    rb = jnp.concatenate([r_b] * reps, axis=1) if reps > 1 else r_b
    a = jnp.exp(lpos + (incl - lneg) + rb)
    if earlier is not None:
        a = jnp.where(earlier, a, 0.0)
    total = jnp.broadcast_to(incl[:, 0:1], r_b.shape)
    return a, total


def _prompt_stick_kernel(it_ref, jt_ref, q_ref, k_ref, v_ref, o_ref, r_ref, acc_ref, suf_ref):
    p = pl.program_id(1)
    i = it_ref[p]
    j = jt_ref[p]

    @pl.when(j == i)
    def _():
        r_ref[...] = jnp.zeros(r_ref.shape, F32)
        acc_ref[...] = jnp.zeros(acc_ref.shape, F32)
        suf_ref[...] = _suffix_matrix(TK)

    row = i * TQ + lax.broadcasted_iota(I32, (TQ, TK), 0)
    col = j * TK + lax.broadcasted_iota(I32, (TQ, TK), 1)
    earlier = col < row
    suf = suf_ref[...]

    def head(h, _):
        z = _nt_dot(q_ref[0, h], k_ref[0, h])
        a, total = _stick_tile(z, earlier, r_ref[h], suf)
        acc_ref[h] = acc_ref[h] + jnp.dot(a.astype(BF16), v_ref[0, h], preferred_element_type=F32)
        r_ref[h] = r_ref[h] + total
        return 0

    lax.fori_loop(0, B_HEADS, head, 0)

    @pl.when(j == 0)
    def _():
        for h in range(B_HEADS):
            o_ref[0, :, h * HEAD_DIM:(h + 1) * HEAD_DIM] = acc_ref[h]


def _prompt_stick(qb, kbh, vbh):
    B, H, L, _ = qb.shape
    nq = L // TQ
    it, jt = _tri_tables(nq, descending=True)
    grid_spec = pltpu.PrefetchScalarGridSpec(
        num_scalar_prefetch=2,
        grid=(B, int(it.shape[0])),
        in_specs=[pl.BlockSpec((1, H, TQ, HEAD_DIM), lambda b, p, it, jt: (b, 0, it[p], 0)),
                  pl.BlockSpec((1, H, TK, HEAD_DIM), lambda b, p, it, jt: (b, 0, jt[p], 0)),
                  pl.BlockSpec((1, H, TK, HEAD_DIM), lambda b, p, it, jt: (b, 0, jt[p], 0))],
        out_specs=pl.BlockSpec((1, TQ, B_WIDTH), lambda b, p, it, jt: (b, it[p], 0)),
        scratch_shapes=[pltpu.VMEM((B_HEADS, TQ, LANES), F32),
                        pltpu.VMEM((B_HEADS, TQ, HEAD_DIM), F32),
                        pltpu.VMEM((TK, TK), BF16)],
    )
    return pl.pallas_call(
        _prompt_stick_kernel,
        grid_spec=grid_spec,
        out_shape=jax.ShapeDtypeStruct((B, L, B_WIDTH), F32),
        compiler_params=pltpu.CompilerParams(
            dimension_semantics=("arbitrary", "arbitrary"), vmem_limit_bytes=VMEM_LIMIT),
        name="prompt_stick",
    )(it, jt, qb, kbh, vbh)


def _merge_kernel(x_ref, oa_ref, ga_ref, ob_ref, gb_ref, ma_ref, mb_ref, woa_ref, wob_ref, wout_ref, fg_ref,
                  o_ref, *, final_norm):
    ga = ga_ref[...]
    gb = gb_ref[...]
    ha = (oa_ref[...] * (ga * _sigmoid(ga))).astype(BF16)
    hb = (ob_ref[...] * (gb * _sigmoid(gb))).astype(BF16)
    pa = jnp.dot(ha, woa_ref[...], preferred_element_type=F32)
    pb = jnp.dot(hb, wob_ref[...], preferred_element_type=F32)
    mix = (_sigmoid(ma_ref[...]) * pa + _sigmoid(mb_ref[...]) * pb).astype(BF16)
    x = x_ref[...] + jnp.dot(mix, wout_ref[...], preferred_element_type=F32)
    if final_norm:
        ms = jnp.mean(x * x, axis=-1, keepdims=True)
        x = x * lax.rsqrt(ms + EPS) * fg_ref[...]
    o_ref[...] = x


def _merge(x, oa, ga, ob, gb, ma, mb, woa, wob, wout, final_g, final_norm, tr):
    R, D = x.shape
    row = lambda w: pl.BlockSpec((tr, w), lambda i: (i, 0))
    full = lambda a: pl.BlockSpec(a.shape, lambda i: (0, 0))
    fg = final_g.reshape(1, D)
    return pl.pallas_call(
        functools.partial(_merge_kernel, final_norm=final_norm),
        grid=(R // tr,),
        in_specs=[row(D), row(A_WIDTH), row(A_WIDTH), row(B_WIDTH), row(B_WIDTH), row(D), row(D),
                  full(woa), full(wob), full(wout), full(fg)],
        out_specs=row(D),
        out_shape=jax.ShapeDtypeStruct((R, D), F32),
        compiler_params=pltpu.CompilerParams(
            dimension_semantics=("arbitrary",), vmem_limit_bytes=VMEM_LIMIT),
        name="merge",
    )(x, oa, ga, ob, gb, ma, mb, woa, wob, wout, fg)


QROWS = SUBLANES


def _sample_score_kernel(pt_ref, iq_ref, wb_ref, page_ref, new_ref, s_ref, *, n_pages):
    j = pl.program_id(1)

    def scores(ik):
        d = _nt_dot(iq_ref[0], ik)
        t = wb_ref[0] * jnp.maximum(d, 0.0)
        s = t[0:QROWS]
        for h in range(1, IDX_HEADS):
            s = s + t[h * QROWS:(h + 1) * QROWS]
        s_ref[0] = s

    @pl.when(j < n_pages)
    def _():
        scores(page_ref[0].astype(BF16))

    @pl.when(j == n_pages)
    def _():
        scores(new_ref[0])


def _sample_scores(page_table, iq_rows, wb_rows, cache_idx, ik_new):
    DB, n_pages = page_table.shape
    P = cache_idx.shape[1]
    R = iq_rows.shape[1]
    grid_spec = pltpu.PrefetchScalarGridSpec(
        num_scalar_prefetch=1,
        grid=(DB, n_pages + 1),
        in_specs=[pl.BlockSpec((1, R, IDX_DIM), lambda b, j, pt: (b, 0, 0)),
                  pl.BlockSpec((1, R, P), lambda b, j, pt: (b, 0, 0)),
                  pl.BlockSpec((1, P, IDX_DIM), lambda b, j, pt: (pt[b, jnp.minimum(j, n_pages - 1)], 0, 0)),
                  pl.BlockSpec((1, P, IDX_DIM), lambda b, j, pt: (b, 0, 0))],
        out_specs=pl.BlockSpec((1, QROWS, P), lambda b, j, pt: (b, 0, j)),
    )
    return pl.pallas_call(
        functools.partial(_sample_score_kernel, n_pages=n_pages),
        grid_spec=grid_spec,
        out_shape=jax.ShapeDtypeStruct((DB, QROWS, (n_pages + 1) * P), F32),
        compiler_params=pltpu.CompilerParams(
            dimension_semantics=("arbitrary", "arbitrary"), vmem_limit_bytes=VMEM_LIMIT),
        name="sample_scores",
    )(page_table, iq_rows, wb_rows, cache_idx, ik_new)


def _sample_select_kernel(s_ref, thr_ref, cidx_ref, keys_ref, *, k_top, past, page):
    nblk = s_ref.shape[2] // page
    row = lax.broadcasted_iota(I32, (QROWS, page), 0)
    lane = lax.broadcasted_iota(I32, (QROWS, page), 1)
    for j in range(nblk):
        vis = (j * page + lane <= past + row) if (j + 1) * page > past else None
        keys_ref[j] = _keys_of(s_ref[0, :, j * page:(j + 1) * page], vis)
    thr, cidx = _select_threshold(keys_ref, nblk, k_top, QROWS, page, nblk * page)
    thr_ref[0] = thr
    cidx_ref[0] = cidx


def _sample_select(scores, k_top, past, page):
    DB, _, S = scores.shape
    col1 = pl.BlockSpec((1, QROWS, 1), lambda b: (b, 0, 0))
    return pl.pallas_call(
        functools.partial(_sample_select_kernel, k_top=k_top, past=past, page=page),
        grid=(DB,),
        in_specs=[pl.BlockSpec((1, QROWS, S), lambda b: (b, 0, 0))],
        out_specs=(col1, col1),
        out_shape=(jax.ShapeDtypeStruct((DB, QROWS, 1), I32), jax.ShapeDtypeStruct((DB, QROWS, 1), I32)),
        scratch_shapes=[pltpu.VMEM((S // page, QROWS, page), I32)],
        compiler_params=pltpu.CompilerParams(
            dimension_semantics=("arbitrary",), vmem_limit_bytes=VMEM_LIMIT),
        name="sample_select",
    )(scores)


def _sample_dsa_kernel(pt_ref, s_ref, thr_ref, cidx_ref, q_ref, page_ref, new_ref, o_ref,
                       m_ref, l_ref, acc_ref, *, n_pages, past, page):
    j = pl.program_id(1)
    R = q_ref.shape[1]

    @pl.when(j == 0)
    def _():
        m_ref[...] = jnp.full(m_ref.shape, NEG_BIG, F32)
        l_ref[...] = jnp.zeros(l_ref.shape, F32)
        acc_ref[...] = jnp.zeros(acc_ref.shape, F32)

    def step(kv, visible):
        col = j * page + lax.broadcasted_iota(I32, (QROWS, page), 1)
        trow = lax.broadcasted_iota(I32, (QROWS, page), 0)
        key = _keys_of(s_ref[0], (col <= past + trow) if visible else None)
        sel8 = _selected(key, col, jnp.broadcast_to(thr_ref[0], (QROWS, page)),
                         jnp.broadcast_to(cidx_ref[0], (QROWS, page)))
        sel = jnp.concatenate([sel8.astype(I32)] * A_HEADS, axis=0) > 0
        lg = _nt_dot(q_ref[0], kv[:, :A_KV_WIDTH])
        lg = jnp.where(sel, lg, 2.0 * NEG_BIG)
        m_prev = m_ref[...]
        m_new = jnp.maximum(m_prev, jnp.max(lg, axis=1, keepdims=True))
        alpha = jnp.exp(m_prev - m_new)
        pr = jnp.exp(lg - m_new)
        l_ref[...] = alpha * l_ref[...] + jnp.sum(pr, axis=1, keepdims=True)
        pv = jnp.dot(pr.astype(BF16), kv[:, A_KV_WIDTH:], preferred_element_type=F32)
        acc_ref[...] = jnp.concatenate([alpha] * (A_KV_WIDTH // LANES), axis=1) * acc_ref[...] + pv
        m_ref[...] = m_new

    @pl.when(j < n_pages)
    def _():
        step(page_ref[0].astype(BF16), False)

    @pl.when(j == n_pages)
    def _():
        step(new_ref[0], True)
        out = acc_ref[...] / jnp.concatenate([l_ref[...]] * (A_KV_WIDTH // LANES), axis=1)
        for h in range(A_HEADS):
            g = h // A_GROUP
            o_ref[0, h] = out[h * QROWS:(h + 1) * QROWS, g * HEAD_DIM:(g + 1) * HEAD_DIM]


def _sample_dsa(page_table, scores, thr, cidx, q_bd, cache_kv, kv_new, past):
    DB, n_pages = page_table.shape
    P = cache_kv.shape[1]
    R = q_bd.shape[1]
    W = cache_kv.shape[2]
    col1 = pl.BlockSpec((1, QROWS, 1), lambda b, j, pt: (b, 0, 0))
    grid_spec = pltpu.PrefetchScalarGridSpec(
        num_scalar_prefetch=1,
        grid=(DB, n_pages + 1),
        in_specs=[pl.BlockSpec((1, QROWS, P), lambda b, j, pt: (b, 0, j)), col1, col1,
                  pl.BlockSpec((1, R, A_KV_WIDTH), lambda b, j, pt: (b, 0, 0)),
                  pl.BlockSpec((1, P, W), lambda b, j, pt: (pt[b, jnp.minimum(j, n_pages - 1)], 0, 0)),
                  pl.BlockSpec((1, P, W), lambda b, j, pt: (b, 0, 0))],
        out_specs=pl.BlockSpec((1, A_HEADS, QROWS, HEAD_DIM), lambda b, j, pt: (b, 0, 0, 0)),
        scratch_shapes=[pltpu.VMEM((R, LANES), F32), pltpu.VMEM((R, LANES), F32),
                        pltpu.VMEM((R, A_KV_WIDTH), F32)],
    )
    return pl.pallas_call(
        functools.partial(_sample_dsa_kernel, n_pages=n_pages, past=past, page=P),
        grid_spec=grid_spec,
        out_shape=jax.ShapeDtypeStruct((DB, A_HEADS, QROWS, HEAD_DIM), F32),
        compiler_params=pltpu.CompilerParams(
            dimension_semantics=("arbitrary", "arbitrary"), vmem_limit_bytes=VMEM_LIMIT),
        name="sample_dsa",
    )(page_table, scores, thr, cidx, q_bd, cache_kv, kv_new)


def _sample_stick_kernel(pt_ref, q_ref, page_ref, new_ref, o_ref, r_ref, acc_ref, suf_ref, *, n_pages, page):
    j = pl.program_id(1)

    @pl.when(j == 0)
    def _():
        r_ref[...] = jnp.zeros(r_ref.shape, F32)
        acc_ref[...] = jnp.zeros(acc_ref.shape, F32)
        suf_ref[...] = _suffix_matrix(page)

    def step(kv, earlier):
        z = _nt_dot(q_ref[0], kv[:, :B_WIDTH])
        a, total = _stick_tile(z, earlier, r_ref[...], suf_ref[...])
        acc_ref[...] = acc_ref[...] + jnp.dot(a.astype(BF16), kv[:, B_WIDTH:], preferred_element_type=F32)
        r_ref[...] = r_ref[...] + total

    @pl.when(j == 0)
    def _():
        R = q_ref.shape[1]
        t = lax.broadcasted_iota(I32, (R, page), 0) % QROWS
        s = lax.broadcasted_iota(I32, (R, page), 1)
        step(new_ref[0], s < t)

    @pl.when(j > 0)
    def _():
        step(page_ref[0].astype(BF16), None)

    @pl.when(j == n_pages)
    def _():
        acc = acc_ref[...]
        for h in range(B_HEADS):
            o_ref[0, h] = acc[h * QROWS:(h + 1) * QROWS, h * HEAD_DIM:(h + 1) * HEAD_DIM]


def _sample_stick(page_table, q_bd, cache_kv, kv_new):
    DB, n_pages = page_table.shape
    P = cache_kv.shape[1]
    R = q_bd.shape[1]
    W = cache_kv.shape[2]
    grid_spec = pltpu.PrefetchScalarGridSpec(
        num_scalar_prefetch=1,
        grid=(DB, n_pages + 1),
        in_specs=[pl.BlockSpec((1, R, B_WIDTH), lambda b, j, pt: (b, 0, 0)),
                  pl.BlockSpec((1, P, W), lambda b, j, pt: (pt[b, n_pages - jnp.maximum(j, 1)], 0, 0)),
                  pl.BlockSpec((1, P, W), lambda b, j, pt: (b, 0, 0))],
        out_specs=pl.BlockSpec((1, B_HEADS, QROWS, HEAD_DIM), lambda b, j, pt: (b, 0, 0, 0)),
        scratch_shapes=[pltpu.VMEM((R, LANES), F32), pltpu.VMEM((R, B_WIDTH), F32),
                        pltpu.VMEM((P, P), BF16)],
    )
    return pl.pallas_call(
        functools.partial(_sample_stick_kernel, n_pages=n_pages, page=P),
        grid_spec=grid_spec,
        out_shape=jax.ShapeDtypeStruct((DB, B_HEADS, QROWS, HEAD_DIM), F32),
        compiler_params=pltpu.CompilerParams(
            dimension_semantics=("arbitrary", "arbitrary"), vmem_limit_bytes=VMEM_LIMIT),
        name="sample_stick",
    )(page_table, q_bd, cache_kv, kv_new)


def _sample_rows(hm, db, ds):
    nh = hm.shape[1]
    a = hm[0].reshape(nh, db, ds, HEAD_DIM)
    a = jnp.pad(a, ((0, 0), (0, 0), (0, QROWS - ds), (0, 0)))
    return jnp.transpose(a, (1, 0, 2, 3)).reshape(db, nh * QROWS, HEAD_DIM)


def _block_diag(rows, nh, group, width_heads):
    db = rows.shape[0]
    r = rows.reshape(db, nh, QROWS, 1, HEAD_DIM)
    onehot = (jnp.arange(nh)[:, None] // group == jnp.arange(width_heads)[None, :])
    out = jnp.where(onehot[None, :, None, :, None], r, jnp.zeros((), rows.dtype))
    return out.reshape(db, nh * QROWS, width_heads * HEAD_DIM)


def _pad_new(tok, db, ds, page):
    w = tok.shape[-1]
    a = tok.reshape(db, ds, w)
    return jnp.pad(a, ((0, 0), (0, page - ds), (0, 0))).astype(BF16)


def _heads_to_tokens(o, ds):
    db, nh = o.shape[0], o.shape[1]
    return jnp.transpose(o[:, :, :ds], (0, 2, 1, 3)).reshape(db * ds, nh * HEAD_DIM)


def kernel(x_prompt, x_sample, cache_a_kv, cache_a_idx, cache_b_kv, page_table,
           meta, norm_g, w_in, w_oa, w_ob, w_out, final_g):
    depth = norm_g.shape[0]
    b, seq, d = x_prompt.shape
    n_meta = meta.shape[0]
    l = n_meta + seq
    blk = math.lcm(TQ, TK, TR)
    lp = -(-l // blk) * blk
    k_top_p = min(TOPK, seq // 4)

    w_pad = [_pad_w_in(w_in[i], d).astype(BF16) for i in range(depth)]
    woa = w_oa.astype(BF16)
    wob = w_ob.astype(BF16)
    wout = w_out.astype(BF16)

    x = jnp.concatenate([jnp.broadcast_to(meta[None].astype(x_prompt.dtype), (b, n_meta, d)), x_prompt], axis=1)
    x = jnp.pad(x, ((0, 0), (0, lp - l), (0, 0)))
    tables_p = _rope_tables(jnp.arange(lp))
    akv_p, aidx_p, bkv_p = [], [], []
    y_prompt = None
    for layer in range(depth):
        pr = _project(x, norm_g[layer], w_pad[layer], tables_p, TR)
        thr, cidx = _prompt_select(pr["iq"], pr["iw"], pr["ikb"], k_top_p)
        oa = _prompt_dsa(pr["iq"], pr["iw"], pr["ikb"], thr, cidx, pr["qa"], pr["kah"], pr["vah"])
        ob = _prompt_stick(pr["qb"], pr["kbh"], pr["vbh"])
        last = layer == depth - 1
        flat = lambda a: a.reshape(b * lp, a.shape[-1])
        xn = _merge(flat(x), flat(oa), flat(pr["ga"]), flat(ob), flat(pr["gb"]), flat(pr["ma"]), flat(pr["mb"]),
                    woa[layer], wob[layer], wout[layer], final_g, last, TR)
        x = xn.reshape(b, lp, d)
        akv_p.append(pr["kva"][:, :l].reshape(b, l, 2, A_KV_HEADS, HEAD_DIM))
        aidx_p.append(pr["ik"][:, :l])
        bkv_p.append(pr["kvb"][:, :l].reshape(b, l, 2, B_HEADS, HEAD_DIM))
    y_prompt = x[:, n_meta:l]

    db, ds, _ = x_sample.shape
    n_pages = page_table.shape[1]
    page = cache_a_idx.shape[2]
    past = n_pages * page
    k_top_s = min(TOPK, (past + ds) // 4)
    rows = db * ds
    rows_p = -(-rows // SUBLANES) * SUBLANES
    tr_s = rows_p if rows_p <= TR else TR
    rows_p = -(-rows_p // tr_s) * tr_s
    pos_s = past + (jnp.arange(rows_p) % ds)
    tables_s = _rope_tables(pos_s)
    xs = jnp.pad(x_sample.reshape(rows, d), ((0, rows_p - rows), (0, 0)))
    akv_s, aidx_s, bkv_s = [], [], []
    for layer in range(depth):
        pr = _project(xs[None], norm_g[layer], w_pad[layer], tables_s, tr_s)
        pr = {k: (v[:, :, :rows] if v.ndim == 4 else v[:, :rows]) for k, v in pr.items()}
        iq_rows = _sample_rows(pr["iq"], db, ds)
        iw = jnp.pad(pr["iw"][0].reshape(db, ds, IDX_HEADS), ((0, 0), (0, QROWS - ds), (0, 0)))
        wb_rows = jnp.broadcast_to(jnp.transpose(iw, (0, 2, 1)).reshape(db, IDX_HEADS * QROWS, 1),
                                   (db, IDX_HEADS * QROWS, page))
        ik_new = _pad_new(pr["ik"], db, ds, page)
        scores = _sample_scores(page_table, iq_rows, wb_rows, cache_a_idx[layer], ik_new)
        thr, cidx = _sample_select(scores, k_top_s, past, page)
        qa_bd = _block_diag(_sample_rows(pr["qa"], db, ds), A_HEADS, A_GROUP, A_KV_HEADS)
        kva_new = _pad_new(pr["kva"], db, ds, page)
        oa = _sample_dsa(page_table, scores, thr, cidx, qa_bd,
                         cache_a_kv[layer].reshape(-1, page, 2 * A_KV_WIDTH), kva_new, past)
        qb_bd = _block_diag(_sample_rows(pr["qb"], db, ds), B_HEADS, 1, B_HEADS)
        kvb_new = _pad_new(pr["kvb"], db, ds, page)
        ob = _sample_stick(page_table, qb_bd, cache_b_kv[layer].reshape(-1, page, 2 * B_WIDTH), kvb_new)
        last = layer == depth - 1
        padr = lambda a: jnp.pad(a, ((0, rows_p - rows), (0, 0)))
        xs = _merge(xs, padr(_heads_to_tokens(oa, ds)), padr(pr["ga"][0]), padr(_heads_to_tokens(ob, ds)),
                    padr(pr["gb"][0]), padr(pr["ma"][0]), padr(pr["mb"][0]),
                    woa[layer], wob[layer], wout[layer], final_g, last, tr_s)
        akv_s.append(pr["kva"][0].reshape(db, ds, 2, A_KV_HEADS, HEAD_DIM))
        aidx_s.append(pr["ik"][0].reshape(db, ds, IDX_DIM))
        bkv_s.append(pr["kvb"][0].reshape(db, ds, 2, B_HEADS, HEAD_DIM))
    y_sample = xs[:rows].reshape(db, ds, d)

    return (y_prompt, y_sample, jnp.stack(akv_p), jnp.stack(aidx_p), jnp.stack(bkv_p),
            jnp.stack(akv_s), jnp.stack(aidx_s), jnp.stack(bkv_s))
```

```python
import functools
import math

import numpy as np
import jax
import jax.numpy as jnp
from jax import lax
from jax.experimental import pallas as pl
from jax.experimental.pallas import tpu as pltpu

F32 = jnp.float32
BF16 = jnp.bfloat16
I32 = jnp.int32

HEAD_DIM = 64
A_HEADS = 8
A_KV_HEADS = 4
A_GROUP = A_HEADS // A_KV_HEADS
IDX_HEADS = 8
IDX_DIM = 64
B_HEADS = 8
TOPK = 256
ROPE_THETA = 500000.0
EPS = 1e-6
A_WIDTH = A_HEADS * HEAD_DIM
A_KV_WIDTH = A_KV_HEADS * HEAD_DIM
B_WIDTH = B_HEADS * HEAD_DIM
IDX_WIDTH = IDX_HEADS * IDX_DIM

LANES = 128
SUBLANES = 8
VMEM_LIMIT = 56 * 1024 * 1024
INT_MIN = -(2 ** 31)
INT_MAX = 2 ** 31 - 1
NEG_BIG = -1e30
LOG2E = 1.4426950408889634
SP_CLAMP = 64.0

TQ = 256
TK = 256
TR = 256
SEL_ROWS = 128
STICK_ROWS = TQ
SEL_COLS = TK
PAGE_GROUP = 8


def _nt_dot(a, b):
    return lax.dot_general(a, b, (((1,), (1,)), ((), ())), preferred_element_type=F32)


def _sigmoid(x):
    return 1.0 / (1.0 + jnp.exp(-x))


_OFF_QA = 0
_OFF_KA = _OFF_QA + A_WIDTH
_OFF_VA = _OFF_KA + A_KV_WIDTH
_OFF_IQ = _OFF_VA + A_KV_WIDTH
_OFF_IK = _OFF_IQ + IDX_WIDTH
_OFF_GA = _OFF_IK + LANES
_OFF_QB = _OFF_GA + A_WIDTH
_OFF_KB = _OFF_QB + B_WIDTH
_OFF_VB = _OFF_KB + B_WIDTH
_OFF_GB = _OFF_VB + B_WIDTH


def _pad_w_in(w, d_model):
    cut = _OFF_IK + IDX_DIM + IDX_HEADS
    pad = LANES - IDX_DIM - IDX_HEADS
    return jnp.concatenate([w[:, :cut], jnp.zeros((w.shape[0], pad), w.dtype), w[:, cut:]], axis=1)


def _rope_tables(pos):
    rot = HEAD_DIM // 4
    half = rot // 2
    inv = ROPE_THETA ** (-jnp.arange(half, dtype=F32) * (2.0 / rot))
    ang = pos.astype(F32)[:, None] * inv[None, :]
    cos = jnp.cos(ang)
    sin = jnp.sin(ang)
    n = pos.shape[0]
    one = jnp.ones((n, HEAD_DIM - rot), F32)
    zero = jnp.zeros((n, HEAD_DIM - rot), F32)
    zh = jnp.zeros((n, half), F32)
    c = jnp.concatenate([cos, cos, one], axis=1)
    sa = jnp.concatenate([-sin, zh, zero], axis=1)
    sb = jnp.concatenate([zh, sin, zero], axis=1)
    two = lambda t: jnp.concatenate([t, t], axis=1)
    return two(c), two(sa), two(sb)


def _rope(y, c, sa, sb):
    half = HEAD_DIM // 8
    return y * c + pltpu.roll(y, LANES - half, 1) * sa + pltpu.roll(y, half, 1) * sb


def _proj_kernel(x_ref, g_ref, w_ref, c_ref, sa_ref, sb_ref,
                 qa_ref, kva_ref, kah_ref, vah_ref, iq_ref, ik_ref, ikb_ref, iw_ref,
                 ga_ref, qb_ref, kvb_ref, kbh_ref, vbh_ref, gb_ref, ma_ref, mb_ref, *, d_model):
    x = x_ref[0]
    ms = jnp.mean(x * x, axis=-1, keepdims=True)
    h = (x * lax.rsqrt(ms + EPS) * g_ref[...]).astype(BF16)
    c = c_ref[...]
    sa = sa_ref[...]
    sb = sb_ref[...]

    def seg(off, width):
        return jnp.dot(h, w_ref[:, off:off + width], preferred_element_type=F32)

    def roped(u, groups):
        return [_rope(u[:, i * LANES:(i + 1) * LANES], c, sa, sb) for i in range(groups)]

    def put_heads(ref, groups, scale):
        for i, y in enumerate(groups):
            ys = (y * scale).astype(ref.dtype) if scale != 1.0 else y.astype(ref.dtype)
            ref[0, 2 * i] = ys[:, :HEAD_DIM]
            ref[0, 2 * i + 1] = ys[:, HEAD_DIM:]

    def split(u):
        return [u[:, i * LANES:(i + 1) * LANES] for i in range(u.shape[1] // LANES)]

    qscale = LOG2E * HEAD_DIM ** -0.5
    put_heads(qa_ref, roped(seg(_OFF_QA, A_WIDTH), A_WIDTH // LANES), qscale)
    ka = roped(seg(_OFF_KA, A_KV_WIDTH), A_KV_WIDTH // LANES)
    va = split(seg(_OFF_VA, A_KV_WIDTH))
    for i, y in enumerate(ka):
        kva_ref[0, :, i * LANES:(i + 1) * LANES] = y
    for i, y in enumerate(va):
        kva_ref[0, :, A_KV_WIDTH + i * LANES:A_KV_WIDTH + (i + 1) * LANES] = y
    put_heads(kah_ref, ka, 1.0)
    put_heads(vah_ref, va, 1.0)
    put_heads(iq_ref, roped(seg(_OFF_IQ, IDX_WIDTH), IDX_WIDTH // LANES), IDX_DIM ** -0.5)
    u = seg(_OFF_IK, LANES)
    ik = _rope(u, c, sa, sb)[:, :IDX_DIM]
    ik_ref[0] = ik
    ikb_ref[0] = ik.astype(BF16)
    iw_ref[0] = u[:, IDX_DIM:IDX_DIM + IDX_HEADS] * (IDX_HEADS ** -0.5)
    ga_ref[0] = seg(_OFF_GA, A_WIDTH)
    gb_ref[0] = seg(_OFF_GB, B_WIDTH)
    put_heads(qb_ref, split(seg(_OFF_QB, B_WIDTH)), qscale)
    kb = seg(_OFF_KB, B_WIDTH)
    vb = seg(_OFF_VB, B_WIDTH)
    kvb_ref[0, :, :B_WIDTH] = kb
    kvb_ref[0, :, B_WIDTH:] = vb
    put_heads(kbh_ref, split(kb), 1.0)
    put_heads(vbh_ref, split(vb), 1.0)
    off_ma = _OFF_GB + B_WIDTH
    ma_ref[0] = seg(off_ma, d_model)
    mb_ref[0] = seg(off_ma + d_model, d_model)


def _project(x, g, w_pad, tables, tr):
    G, L, D = x.shape
    c, sa, sb = tables
    wn = w_pad.shape[1]
    f = lambda w: jax.ShapeDtypeStruct((G, L, w), F32)
    hm = lambda nh: jax.ShapeDtypeStruct((G, nh, L, HEAD_DIM), BF16)
    out_shape = (
        hm(A_HEADS), f(2 * A_KV_WIDTH), hm(A_KV_HEADS), hm(A_KV_HEADS),
        hm(IDX_HEADS), f(IDX_DIM), jax.ShapeDtypeStruct((G, L, IDX_DIM), BF16), f(IDX_HEADS),
        f(A_WIDTH), hm(B_HEADS), f(2 * B_WIDTH), hm(B_HEADS), hm(B_HEADS), f(B_WIDTH), f(D), f(D),
    )
    row = lambda w: pl.BlockSpec((1, tr, w), lambda b, i: (b, i, 0))
    hms = lambda nh: pl.BlockSpec((1, nh, tr, HEAD_DIM), lambda b, i: (b, 0, i, 0))
    tab = pl.BlockSpec((tr, LANES), lambda b, i: (i, 0))
    out_specs = (
        hms(A_HEADS), row(2 * A_KV_WIDTH), hms(A_KV_HEADS), hms(A_KV_HEADS),
        hms(IDX_HEADS), row(IDX_DIM), row(IDX_DIM), row(IDX_HEADS),
        row(A_WIDTH), hms(B_HEADS), row(2 * B_WIDTH), hms(B_HEADS), hms(B_HEADS), row(B_WIDTH), row(D), row(D),
    )
    names = ("qa", "kva", "kah", "vah", "iq", "ik", "ikb", "iw",
             "ga", "qb", "kvb", "kbh", "vbh", "gb", "ma", "mb")
    outs = pl.pallas_call(
        functools.partial(_proj_kernel, d_model=D),
        grid=(G, L // tr),
        in_specs=[row(D), pl.BlockSpec((1, D), lambda b, i: (0, 0)),
                  pl.BlockSpec((D, wn), lambda b, i: (0, 0)), tab, tab, tab],
        out_specs=out_specs,
        out_shape=out_shape,
        compiler_params=pltpu.CompilerParams(
            dimension_semantics=("arbitrary", "arbitrary"), vmem_limit_bytes=VMEM_LIMIT),
        name="proj",
    )(x, g.reshape(1, D), w_pad, c, sa, sb)
    return dict(zip(names, outs))


def _score_keys(iq_ref, wb_ref, ik, visible):
    reps = ik.shape[0] // LANES
    s = None
    for h in range(IDX_HEADS):
        w = wb_ref[h]
        if reps > 1:
            w = jnp.concatenate([w] * reps, axis=1)
        t = w * jnp.maximum(_nt_dot(iq_ref[h], ik), 0.0)
        s = t if s is None else s + t
    return _keys_of(s, visible)


def _keys_of(s, visible):
    s = s + 0.0
    bits = pltpu.bitcast(s, I32)
    key = bits ^ ((bits >> 31) & INT_MAX)
    if visible is not None:
        key = jnp.where(visible, key, INT_MIN)
    return key


def _bcast_w(iw, wb_ref):
    for h in range(IDX_HEADS):
        wb_ref[h] = jnp.broadcast_to(iw[:, h:h + 1], (iw.shape[0], LANES))


def _select_threshold(load, npairs, k_top, tq, tk, max_cols):
    def count(pred):
        def body(jj, acc):
            for u in range(2):
                j = 2 * jj + u
                acc = acc + jnp.where(pred(j, load(j)), 1, 0)
            return acc

        acc = lax.fori_loop(0, npairs, body, jnp.zeros((tq, tk), I32))
        return jnp.sum(acc, axis=1, keepdims=True)

    def bit_body(it, prefix):
        bit = lax.shift_left(jnp.int32(1), jnp.int32(31) - it)
        cand = prefix + bit
        cb = jnp.broadcast_to(cand, (tq, tk))
        return jnp.where(count(lambda j, kj: kj >= cb) >= k_top, cand, prefix)

    thr = lax.fori_loop(0, 32, bit_body, jnp.full((tq, 1), INT_MIN, I32))
    tb = jnp.broadcast_to(thr, (tq, tk))
    n_gt = count(lambda j, kj: kj > tb)
    n_eq = count(lambda j, kj: kj == tb)
    need = k_top - n_gt
    few = thr == INT_MIN
    excess = jnp.logical_and(n_eq > need, jnp.logical_not(few))
    nbits = max(1, int(max_cols).bit_length())
    lane = lax.broadcasted_iota(I32, (tq, tk), 1)

    def tie_search(_):
        def bit_body2(it, lo):
            cand = lo + lax.shift_left(jnp.int32(1), jnp.int32(nbits - 1) - it)
            cb = jnp.broadcast_to(cand, (tq, tk))
            n = count(lambda j, kj: jnp.logical_and(kj == tb, j * tk + lane < cb))
            return jnp.where(n < need, cand, lo)

        return lax.fori_loop(0, nbits, bit_body2, jnp.zeros((tq, 1), I32))

    any_excess = jnp.max(jnp.where(excess, 1, 0)) > 0
    cidx = lax.cond(any_excess, tie_search, lambda _: jnp.zeros((tq, 1), I32), 0)
    cidx = jnp.where(excess, cidx, INT_MAX)
    cidx = jnp.where(few, -1, cidx)
    return thr, cidx


def _selected(key, col, thr_b, cidx_b):
    return jnp.logical_or(key > thr_b, jnp.logical_and(key == thr_b, col <= cidx_b))


def _prompt_select_kernel(iq_ref, iw_ref, ik_ref, thr_ref, cidx_ref, keys_ref, wb_ref, *, k_top):
    i = pl.program_id(1)
    _bcast_w(iw_ref[0], wb_ref)
    row = i * TQ + lax.broadcasted_iota(I32, (TQ, TK), 0)
    lane = lax.broadcasted_iota(I32, (TQ, TK), 1)

    def fill(j, _):
        ik = ik_ref[0, pl.ds(pl.multiple_of(j * TK, TK), TK), :]
        keys_ref[j] = _score_keys(iq_ref.at[0], wb_ref, ik, (j * TK + lane) <= row)
        return 0

    nblk = i + 1
    lax.fori_loop(0, nblk, fill, 0)
    keys_ref[nblk] = jnp.full((TQ, TK), INT_MIN, I32)
    npairs = (nblk + 1) // 2

    def chunk(c, _):
        r0 = pl.multiple_of(c * SEL_ROWS, SEL_ROWS)
        load = lambda j: keys_ref[j, pl.ds(r0, SEL_ROWS), :]
        thr, cidx = _select_threshold(load, npairs, k_top, SEL_ROWS, TK, keys_ref.shape[0] * TK)
        thr_ref[0, pl.ds(r0, SEL_ROWS), :] = thr
        cidx_ref[0, pl.ds(r0, SEL_ROWS), :] = cidx
        return 0

    lax.fori_loop(0, TQ // SEL_ROWS, chunk, 0)


def _prompt_select(iq, iw, ikb, k_top):
    B, H, L, _ = iq.shape
    nq = L // TQ
    assert TQ == TK and TQ % SEL_ROWS == 0
    col1 = pl.BlockSpec((1, TQ, 1), lambda b, i: (b, i, 0))
    return pl.pallas_call(
        functools.partial(_prompt_select_kernel, k_top=k_top),
        grid=(B, nq),
        in_specs=[pl.BlockSpec((1, H, TQ, IDX_DIM), lambda b, i: (b, 0, i, 0)),
                  pl.BlockSpec((1, TQ, IDX_HEADS), lambda b, i: (b, i, 0)),
                  pl.BlockSpec((1, L, IDX_DIM), lambda b, i: (b, 0, 0))],
        out_specs=(col1, col1),
        out_shape=(jax.ShapeDtypeStruct((B, L, 1), I32), jax.ShapeDtypeStruct((B, L, 1), I32)),
        scratch_shapes=[pltpu.VMEM((L // TK + 1, TQ, TK), I32), pltpu.VMEM((IDX_HEADS, TQ, LANES), F32)],
        compiler_params=pltpu.CompilerParams(
            dimension_semantics=("arbitrary", "arbitrary"), vmem_limit_bytes=VMEM_LIMIT),
        name="prompt_select",
    )(iq, iw, ikb)


def _tri_tables(nq, descending):
    it, jt = [], []
    for i in range(nq):
        js = range(i, -1, -1) if descending else range(i + 1)
        for j in js:
            it.append(i)
            jt.append(j)
    return jnp.asarray(np.array(it, np.int32)), jnp.asarray(np.array(jt, np.int32))


def _prompt_dsa_kernel(it_ref, jt_ref, iq_ref, iw_ref, ik_ref, thr_ref, cidx_ref, q_ref, k_ref, v_ref,
                       o_ref, wb_ref, m_ref, l_ref, acc_ref):
    p = pl.program_id(1)
    i = it_ref[p]
    j = jt_ref[p]

    @pl.when(j == 0)
    def _():
        _bcast_w(iw_ref[0], wb_ref)
        m_ref[...] = jnp.full(m_ref.shape, NEG_BIG, F32)
        l_ref[...] = jnp.zeros(l_ref.shape, F32)
        acc_ref[...] = jnp.zeros(acc_ref.shape, F32)

    row = i * TQ + lax.broadcasted_iota(I32, (TQ, TK), 0)
    col = j * TK + lax.broadcasted_iota(I32, (TQ, TK), 1)
    key = _score_keys(iq_ref.at[0], wb_ref, ik_ref[0], col <= row)
    sel = _selected(key, col, jnp.broadcast_to(thr_ref[0], (TQ, TK)), jnp.broadcast_to(cidx_ref[0], (TQ, TK)))
    reps = TK // LANES

    for h in range(A_HEADS):
        g = h // A_GROUP
        s = _nt_dot(q_ref[0, h], k_ref[0, g])
        s = jnp.where(sel, s, 2.0 * NEG_BIG)
        m_prev = m_ref[h]
        m_new = jnp.maximum(m_prev, jnp.max(s, axis=1, keepdims=True))
        alpha = jnp.exp2(m_prev - m_new)
        pr = jnp.exp2(s - jnp.concatenate([m_new] * reps, axis=1))
        l_ref[h] = alpha * l_ref[h] + jnp.sum(pr, axis=1, keepdims=True)
        acc_ref[h] = alpha[:, :HEAD_DIM] * acc_ref[h] + jnp.dot(
            pr.astype(BF16), v_ref[0, g], preferred_element_type=F32)
        m_ref[h] = m_new

    @pl.when(j == i)
    def _():
        for h in range(A_HEADS):
            o_ref[0, :, h * HEAD_DIM:(h + 1) * HEAD_DIM] = acc_ref[h] / l_ref[h][:, :HEAD_DIM]


def _prompt_dsa(iq, iw, ikb, thr, cidx, qa, kah, vah):
    B, H, L, _ = qa.shape
    nq = L // TQ
    it, jt = _tri_tables(nq, descending=False)
    qspec = lambda nh: pl.BlockSpec((1, nh, TQ, HEAD_DIM), lambda b, p, it, jt: (b, 0, it[p], 0))
    kspec = lambda nh: pl.BlockSpec((1, nh, TK, HEAD_DIM), lambda b, p, it, jt: (b, 0, jt[p], 0))
    qrow = lambda w: pl.BlockSpec((1, TQ, w), lambda b, p, it, jt: (b, it[p], 0))
    grid_spec = pltpu.PrefetchScalarGridSpec(
        num_scalar_prefetch=2,
        grid=(B, int(it.shape[0])),
        in_specs=[qspec(IDX_HEADS), qrow(IDX_HEADS),
                  pl.BlockSpec((1, TK, IDX_DIM), lambda b, p, it, jt: (b, jt[p], 0)),
                  qrow(1), qrow(1), qspec(A_HEADS), kspec(A_KV_HEADS), kspec(A_KV_HEADS)],
        out_specs=qrow(A_WIDTH),
        scratch_shapes=[pltpu.VMEM((IDX_HEADS, TQ, LANES), F32),
                        pltpu.VMEM((A_HEADS, TQ, LANES), F32),
                        pltpu.VMEM((A_HEADS, TQ, LANES), F32),
                        pltpu.VMEM((A_HEADS, TQ, HEAD_DIM), F32)],
    )
    return pl.pallas_call(
        _prompt_dsa_kernel,
        grid_spec=grid_spec,
        out_shape=jax.ShapeDtypeStruct((B, L, A_WIDTH), F32),
        compiler_params=pltpu.CompilerParams(
            dimension_semantics=("arbitrary", "arbitrary"), vmem_limit_bytes=VMEM_LIMIT),
        name="prompt_dsa",
    )(it, jt, iq, iw, ikb, thr, cidx, qa, kah, vah)


def _suffix_matrix(n):
    r = lax.broadcasted_iota(I32, (2 * n, n), 0)
    r = jnp.where(r >= n, r - n, r)
    c = lax.broadcasted_iota(I32, (2 * n, n), 1)
    return jnp.where(r >= c, 1.0, 0.0).astype(BF16)


def _stick_tile(z, earlier, r_b, suf):
    p = jnp.exp2(jnp.minimum(z, SP_CLAMP))
    sp = jnp.maximum(z, jnp.log(1.0 + p) * LOG2E)
    if earlier is not None:
        sp = jnp.where(earlier, sp, 0.0)
    hi = sp.astype(BF16)
    lo = (sp - hi.astype(F32)).astype(BF16)
    incl = jnp.dot(jnp.concatenate([hi, lo], axis=1), suf, preferred_element_type=F32)
    reps = z.shape[1] // LANES
    rb = jnp.concatenate([r_b] * reps, axis=1) if reps > 1 else r_b
    a = jnp.exp2((z - incl) - rb)
    if earlier is not None:
        a = jnp.where(earlier, a, 0.0)
    total = jnp.broadcast_to(incl[:, 0:1], r_b.shape)
    return a, total


def _prompt_stick_kernel(it_ref, jt_ref, q_ref, k_ref, v_ref, o_ref, r_ref, acc_ref, suf_ref):
    p = pl.program_id(1)
    i = it_ref[p]
    j = jt_ref[p]

    @pl.when(j == i)
    def _():
        r_ref[...] = jnp.zeros(r_ref.shape, F32)
        acc_ref[...] = jnp.zeros(acc_ref.shape, F32)
        suf_ref[...] = _suffix_matrix(TK)

    def heads(diagonal):
        for h in range(B_HEADS):
            for c in range(TQ // STICK_ROWS):
                rows = slice(c * STICK_ROWS, (c + 1) * STICK_ROWS)
                earlier = None
                if diagonal:
                    row = c * STICK_ROWS + lax.broadcasted_iota(I32, (STICK_ROWS, TK), 0)
                    earlier = lax.broadcasted_iota(I32, (STICK_ROWS, TK), 1) < row
                z = _nt_dot(q_ref[0, h, rows, :], k_ref[0, h])
                a, total = _stick_tile(z, earlier, r_ref[h, rows, :], suf_ref[...])
                acc_ref[h, rows, :] = acc_ref[h, rows, :] + jnp.dot(
                    a.astype(BF16), v_ref[0, h], preferred_element_type=F32)
                r_ref[h, rows, :] = r_ref[h, rows, :] + total

    @pl.when(j == i)
    def _():
        heads(True)

    @pl.when(j < i)
    def _():
        heads(False)

    @pl.when(j == 0)
    def _():
        for h in range(B_HEADS):
            o_ref[0, :, h * HEAD_DIM:(h + 1) * HEAD_DIM] = acc_ref[h]


def _prompt_stick(qb, kbh, vbh):
    B, H, L, _ = qb.shape
    nq = L // TQ
    it, jt = _tri_tables(nq, descending=True)
    grid_spec = pltpu.PrefetchScalarGridSpec(
        num_scalar_prefetch=2,
        grid=(B, int(it.shape[0])),
        in_specs=[pl.BlockSpec((1, H, TQ, HEAD_DIM), lambda b, p, it, jt: (b, 0, it[p], 0)),
                  pl.BlockSpec((1, H, TK, HEAD_DIM), lambda b, p, it, jt: (b, 0, jt[p], 0)),
                  pl.BlockSpec((1, H, TK, HEAD_DIM), lambda b, p, it, jt: (b, 0, jt[p], 0))],
        out_specs=pl.BlockSpec((1, TQ, B_WIDTH), lambda b, p, it, jt: (b, it[p], 0)),
        scratch_shapes=[pltpu.VMEM((B_HEADS, TQ, LANES), F32),
                        pltpu.VMEM((B_HEADS, TQ, HEAD_DIM), F32),
                        pltpu.VMEM((2 * TK, TK), BF16)],
    )
    return pl.pallas_call(
        _prompt_stick_kernel,
        grid_spec=grid_spec,
        out_shape=jax.ShapeDtypeStruct((B, L, B_WIDTH), F32),
        compiler_params=pltpu.CompilerParams(
            dimension_semantics=("arbitrary", "arbitrary"), vmem_limit_bytes=VMEM_LIMIT),
        name="prompt_stick",
    )(it, jt, qb, kbh, vbh)


def _merge_kernel(x_ref, oa_ref, ga_ref, ob_ref, gb_ref, ma_ref, mb_ref, woa_ref, wob_ref, wout_ref, fg_ref,
                  o_ref, *, final_norm):
    ga = ga_ref[...]
    gb = gb_ref[...]
    ha = (oa_ref[...] * (ga * _sigmoid(ga))).astype(BF16)
    hb = (ob_ref[...] * (gb * _sigmoid(gb))).astype(BF16)
    pa = jnp.dot(ha, woa_ref[...], preferred_element_type=F32)
    pb = jnp.dot(hb, wob_ref[...], preferred_element_type=F32)
    mix = (_sigmoid(ma_ref[...]) * pa + _sigmoid(mb_ref[...]) * pb).astype(BF16)
    x = x_ref[...] + jnp.dot(mix, wout_ref[...], preferred_element_type=F32)
    if final_norm:
        ms = jnp.mean(x * x, axis=-1, keepdims=True)
        x = x * lax.rsqrt(ms + EPS) * fg_ref[...]
    o_ref[...] = x


def _merge(x, oa, ga, ob, gb, ma, mb, woa, wob, wout, final_g, final_norm, tr):
    R, D = x.shape
    row = lambda w: pl.BlockSpec((tr, w), lambda i: (i, 0))
    full = lambda a: pl.BlockSpec(a.shape, lambda i: (0, 0))
    fg = final_g.reshape(1, D)
    return pl.pallas_call(
        functools.partial(_merge_kernel, final_norm=final_norm),
        grid=(R // tr,),
        in_specs=[row(D), row(A_WIDTH), row(A_WIDTH), row(B_WIDTH), row(B_WIDTH), row(D), row(D),
                  full(woa), full(wob), full(wout), full(fg)],
        out_specs=row(D),
        out_shape=jax.ShapeDtypeStruct((R, D), F32),
        compiler_params=pltpu.CompilerParams(
            dimension_semantics=("arbitrary",), vmem_limit_bytes=VMEM_LIMIT),
        name="merge",
    )(x, oa, ga, ob, gb, ma, mb, woa, wob, wout, fg)


QROWS = SUBLANES


def _page_specs(n_group, n_pages, block, layer, reverse):
    n_steps = n_pages // n_group
    zeros = (0,) * (len(block) - 2)

    def spec(g):
        if reverse:
            idx = lambda b, j, pt: (layer, pt[b, n_pages - jnp.maximum(j, 1) * n_group + g]) + zeros
        else:
            idx = lambda b, j, pt: (layer, pt[b, jnp.minimum(j, n_steps - 1) * n_group + g]) + zeros
        return pl.BlockSpec(block, idx)

    return [spec(g) for g in range(n_group)]


def _sample_score_kernel(pt_ref, iq_ref, wb_ref, *refs, n_group, n_steps):
    page_refs, new_ref, s_ref = refs[:n_group], refs[n_group], refs[n_group + 1]
    j = pl.program_id(1)
    P = new_ref.shape[2]

    def scores(ikt):
        d = jnp.dot(iq_ref[0], ikt, preferred_element_type=F32)
        t = wb_ref[0] * jnp.maximum(d, 0.0)
        s = t[0:QROWS]
        for h in range(1, IDX_HEADS):
            s = s + t[h * QROWS:(h + 1) * QROWS]
        return s

    def put(g, s):
        per = SEL_COLS // P
        s_ref[g // per, :, (g % per) * P:(g % per + 1) * P] = s

    @pl.when(j < n_steps)
    def _():
        for g in range(n_group):
            put(g, scores(page_refs[g][0, 0].astype(BF16)))

    @pl.when(j == n_steps)
    def _():
        put(0, scores(new_ref[0]))
        for g in range(1, n_group):
            put(g, jnp.zeros((QROWS, P), F32))


def _sample_scores(page_table, iq_rows, wb_rows, cache_idx_t, ikt_new, layer, n_group):
    DB, n_pages = page_table.shape
    P = cache_idx_t.shape[3]
    R = iq_rows.shape[1]
    n_steps = n_pages // n_group
    blk = n_group * P // SEL_COLS
    grid_spec = pltpu.PrefetchScalarGridSpec(
        num_scalar_prefetch=1,
        grid=(DB, n_steps + 1),
        in_specs=[pl.BlockSpec((1, R, IDX_DIM), lambda b, j, pt: (b, 0, 0)),
                  pl.BlockSpec((1, R, P), lambda b, j, pt: (b, 0, 0))]
                 + _page_specs(n_group, n_pages, (1, 1, IDX_DIM, P), layer, False)
                 + [pl.BlockSpec((1, IDX_DIM, P), lambda b, j, pt: (b, 0, 0))],
        out_specs=pl.BlockSpec((blk, QROWS, SEL_COLS), lambda b, j, pt: (j, b, 0)),
    )
    return pl.pallas_call(
        functools.partial(_sample_score_kernel, n_group=n_group, n_steps=n_steps),
        grid_spec=grid_spec,
        out_shape=jax.ShapeDtypeStruct(((n_steps + 1) * blk, DB * QROWS, SEL_COLS), F32),
        compiler_params=pltpu.CompilerParams(
            dimension_semantics=("arbitrary", "arbitrary"), vmem_limit_bytes=VMEM_LIMIT),
        name="sample_scores",
    )(page_table, iq_rows, wb_rows, *([cache_idx_t] * n_group), ikt_new)


def _sample_visible(j, rows, past):
    lane = lax.broadcasted_iota(I32, (rows, SEL_COLS), 1)
    t = lax.broadcasted_iota(I32, (rows, SEL_COLS), 0) % QROWS
    return j * SEL_COLS + lane <= past + t


def _sample_select_kernel(s_ref, thr_ref, cidx_ref, keys_ref, *, k_top, past):
    nblk = s_ref.shape[0]

    def fill(j, _):
        keys_ref[j] = _keys_of(s_ref[j], _sample_visible(j, SEL_ROWS, past))
        return 0

    lax.fori_loop(0, nblk, fill, 0)
    if nblk % 2:
        keys_ref[nblk] = jnp.full((SEL_ROWS, SEL_COLS), INT_MIN, I32)
    load = lambda j: keys_ref[j]
    thr, cidx = _select_threshold(load, (nblk + 1) // 2, k_top, SEL_ROWS, SEL_COLS, keys_ref.shape[0] * SEL_COLS)
    thr_ref[...] = thr
    cidx_ref[...] = cidx


def _sample_select(scores, k_top, past):
    nblk, rows, _ = scores.shape
    assert rows % SEL_ROWS == 0
    col1 = pl.BlockSpec((SEL_ROWS, 1), lambda c: (c, 0))
    return pl.pallas_call(
        functools.partial(_sample_select_kernel, k_top=k_top, past=past),
        grid=(rows // SEL_ROWS,),
        in_specs=[pl.BlockSpec((nblk, SEL_ROWS, SEL_COLS), lambda c: (0, c, 0))],
        out_specs=(col1, col1),
        out_shape=(jax.ShapeDtypeStruct((rows, 1), I32), jax.ShapeDtypeStruct((rows, 1), I32)),
        scratch_shapes=[pltpu.VMEM((nblk + nblk % 2, SEL_ROWS, SEL_COLS), I32)],
        compiler_params=pltpu.CompilerParams(
            dimension_semantics=("arbitrary",), vmem_limit_bytes=VMEM_LIMIT),
        name="sample_select",
    )(scores)


def _sample_dsa_kernel(pt_ref, s_ref, thr_ref, cidx_ref, q_ref, *refs, n_group, n_steps, past):
    page_refs, new_ref, o_ref, m_ref, l_ref, acc_ref = refs[:n_group], *refs[n_group:]
    j = pl.program_id(1)
    P = new_ref.shape[3]
    nblk = s_ref.shape[0]
    wide = A_KV_WIDTH // LANES

    @pl.when(j == 0)
    def _():
        m_ref[...] = jnp.full(m_ref.shape, NEG_BIG, F32)
        l_ref[...] = jnp.zeros(l_ref.shape, F32)
        acc_ref[...] = jnp.zeros(acc_ref.shape, F32)

    def step(kts, vts):
        thr_b = jnp.broadcast_to(thr_ref[...], (QROWS, SEL_COLS))
        cidx_b = jnp.broadcast_to(cidx_ref[...], (QROWS, SEL_COLS))
        lane = lax.broadcasted_iota(I32, (QROWS, SEL_COLS), 1)
        sels = []
        for u in range(nblk):
            jb = j * nblk + u
            key = _keys_of(s_ref[u], _sample_visible(jb, QROWS, past))
            sels.append(_selected(key, jb * SEL_COLS + lane, thr_b, cidx_b).astype(I32))
        sel8 = jnp.concatenate(sels, axis=1)
        sel = jnp.concatenate([sel8] * A_HEADS, axis=0) > 0
        lg = jnp.concatenate([jnp.dot(q_ref[0], kt, preferred_element_type=F32) for kt in kts], axis=1)
        lg = jnp.where(sel, lg, 2.0 * NEG_BIG)
        m_prev = m_ref[...]
        m_new = jnp.maximum(m_prev, jnp.max(lg, axis=1, keepdims=True))
        alpha = jnp.exp2(m_prev - m_new)
        pr = jnp.exp2(lg - jnp.concatenate([m_new] * len(kts), axis=1))
        l_ref[...] = alpha * l_ref[...] + jnp.sum(pr, axis=1, keepdims=True)
        pv = None
        for g, vt in enumerate(vts):
            t = _nt_dot(pr[:, g * P:(g + 1) * P].astype(BF16), vt)
            pv = t if pv is None else pv + t
        acc_ref[...] = jnp.concatenate([alpha] * wide, axis=1) * acc_ref[...] + pv
        m_ref[...] = m_new

    @pl.when(j < n_steps)
    def _():
        step([r[0, 0, 0].astype(BF16) for r in page_refs], [r[0, 0, 1].astype(BF16) for r in page_refs])

    @pl.when(j == n_steps)
    def _():
        zero = jnp.zeros((A_KV_WIDTH, P), BF16)
        step([new_ref[0, 0]] + [zero] * (n_group - 1), [new_ref[0, 1]] + [zero] * (n_group - 1))
        out = acc_ref[...] / jnp.concatenate([l_ref[...]] * wide, axis=1)
        for h in range(A_HEADS):
            g = h // A_GROUP
            o_ref[0, h] = out[h * QROWS:(h + 1) * QROWS, g * HEAD_DIM:(g + 1) * HEAD_DIM]


def _sample_dsa(page_table, scores, thr, cidx, q_bd, cache_kv_t, kvt_new, past, layer, n_group):
    DB, n_pages = page_table.shape
    P = cache_kv_t.shape[4]
    R = q_bd.shape[1]
    n_steps = n_pages // n_group
    blk = n_group * P // SEL_COLS
    col1 = pl.BlockSpec((QROWS, 1), lambda b, j, pt: (b, 0))
    grid_spec = pltpu.PrefetchScalarGridSpec(
        num_scalar_prefetch=1,
        grid=(DB, n_steps + 1),
        in_specs=[pl.BlockSpec((blk, QROWS, SEL_COLS), lambda b, j, pt: (j, b, 0)), col1, col1,
                  pl.BlockSpec((1, R, A_KV_WIDTH), lambda b, j, pt: (b, 0, 0))]
                 + _page_specs(n_group, n_pages, (1, 1, 2, A_KV_WIDTH, P), layer, False)
                 + [pl.BlockSpec((1, 2, A_KV_WIDTH, P), lambda b, j, pt: (b, 0, 0, 0))],
        out_specs=pl.BlockSpec((1, A_HEADS, QROWS, HEAD_DIM), lambda b, j, pt: (b, 0, 0, 0)),
        scratch_shapes=[pltpu.VMEM((R, LANES), F32), pltpu.VMEM((R, LANES), F32),
                        pltpu.VMEM((R, A_KV_WIDTH), F32)],
    )
    return pl.pallas_call(
        functools.partial(_sample_dsa_kernel, n_group=n_group, n_steps=n_steps, past=past),
        grid_spec=grid_spec,
        out_shape=jax.ShapeDtypeStruct((DB, A_HEADS, QROWS, HEAD_DIM), F32),
        compiler_params=pltpu.CompilerParams(
            dimension_semantics=("arbitrary", "arbitrary"), vmem_limit_bytes=VMEM_LIMIT),
        name="sample_dsa",
    )(page_table, scores, thr, cidx, q_bd, *([cache_kv_t] * n_group), kvt_new)


def _sample_stick_kernel(pt_ref, q_ref, *refs, n_group, n_steps):
    page_refs, new_ref, o_ref, r_ref, acc_ref, suf_ref = refs[:n_group], *refs[n_group:]
    j = pl.program_id(1)
    P = new_ref.shape[3]

    @pl.when(j == 0)
    def _():
        r_ref[...] = jnp.zeros(r_ref.shape, F32)
        acc_ref[...] = jnp.zeros(acc_ref.shape, F32)
        suf_ref[...] = _suffix_matrix(P)

    def page(kt, vt, earlier):
        z = jnp.dot(q_ref[0], kt, preferred_element_type=F32)
        a, total = _stick_tile(z, earlier, r_ref[...], suf_ref[...])
        acc_ref[...] = acc_ref[...] + _nt_dot(a.astype(BF16), vt)
        r_ref[...] = r_ref[...] + total

    @pl.when(j == 0)
    def _():
        R = q_ref.shape[1]
        t = lax.broadcasted_iota(I32, (R, P), 0) % QROWS
        s = lax.broadcasted_iota(I32, (R, P), 1)
        page(new_ref[0, 0], new_ref[0, 1], s < t)

    @pl.when(j > 0)
    def _():
        for g in range(n_group - 1, -1, -1):
            page(page_refs[g][0, 0, 0].astype(BF16), page_refs[g][0, 0, 1].astype(BF16), None)

    @pl.when(j == n_steps)
    def _():
        acc = acc_ref[...]
        for h in range(B_HEADS):
            o_ref[0, h] = acc[h * QROWS:(h + 1) * QROWS, h * HEAD_DIM:(h + 1) * HEAD_DIM]


def _sample_stick(page_table, q_bd, cache_kv_t, kvt_new, layer, n_group):
    DB, n_pages = page_table.shape
    P = cache_kv_t.shape[4]
    R = q_bd.shape[1]
    n_steps = n_pages // n_group
    grid_spec = pltpu.PrefetchScalarGridSpec(
        num_scalar_prefetch=1,
        grid=(DB, n_steps + 1),
        in_specs=[pl.BlockSpec((1, R, B_WIDTH), lambda b, j, pt: (b, 0, 0))]
                 + _page_specs(n_group, n_pages, (1, 1, 2, B_WIDTH, P), layer, True)
                 + [pl.BlockSpec((1, 2, B_WIDTH, P), lambda b, j, pt: (b, 0, 0, 0))],
        out_specs=pl.BlockSpec((1, B_HEADS, QROWS, HEAD_DIM), lambda b, j, pt: (b, 0, 0, 0)),
        scratch_shapes=[pltpu.VMEM((R, LANES), F32), pltpu.VMEM((R, B_WIDTH), F32),
                        pltpu.VMEM((2 * P, P), BF16)],
    )
    return pl.pallas_call(
        functools.partial(_sample_stick_kernel, n_group=n_group, n_steps=n_steps),
        grid_spec=grid_spec,
        out_shape=jax.ShapeDtypeStruct((DB, B_HEADS, QROWS, HEAD_DIM), F32),
        compiler_params=pltpu.CompilerParams(
            dimension_semantics=("arbitrary", "arbitrary"), vmem_limit_bytes=VMEM_LIMIT),
        name="sample_stick",
    )(page_table, q_bd, *([cache_kv_t] * n_group), kvt_new)


def _sample_rows(hm, db, ds):
    nh = hm.shape[1]
    a = hm[0].reshape(nh, db, ds, HEAD_DIM)
    a = jnp.pad(a, ((0, 0), (0, 0), (0, QROWS - ds), (0, 0)))
    return jnp.transpose(a, (1, 0, 2, 3)).reshape(db, nh * QROWS, HEAD_DIM)


def _block_diag(rows, nh, group, width_heads):
    db = rows.shape[0]
    r = rows.reshape(db, nh, QROWS, 1, HEAD_DIM)
    onehot = (jnp.arange(nh)[:, None] // group == jnp.arange(width_heads)[None, :])
    out = jnp.where(onehot[None, :, None, :, None], r, jnp.zeros((), rows.dtype))
    return out.reshape(db, nh * QROWS, width_heads * HEAD_DIM)


def _new_t(tok, db, ds, page, parts):
    w = tok.shape[-1]
    a = jnp.transpose(tok.reshape(db, ds, parts, w // parts), (0, 2, 3, 1))
    return jnp.pad(a, ((0, 0), (0, 0), (0, 0), (0, page - ds))).astype(BF16)


def _heads_to_tokens(o, ds):
    db, nh = o.shape[0], o.shape[1]
    return jnp.transpose(o[:, :, :ds], (0, 2, 1, 3)).reshape(db * ds, nh * HEAD_DIM)


def kernel(x_prompt, x_sample, cache_a_kv, cache_a_idx, cache_b_kv, page_table,
           meta, norm_g, w_in, w_oa, w_ob, w_out, final_g):
    depth = norm_g.shape[0]
    b, seq, d = x_prompt.shape
    n_meta = meta.shape[0]
    l = n_meta + seq
    blk = math.lcm(TQ, TK, TR)
    lp = -(-l // blk) * blk
    k_top_p = min(TOPK, seq // 4)

    w_pad = [_pad_w_in(w_in[i], d).astype(BF16) for i in range(depth)]
    woa = w_oa.astype(BF16)
    wob = w_ob.astype(BF16)
    wout = w_out.astype(BF16)

    x = jnp.concatenate([jnp.broadcast_to(meta[None].astype(x_prompt.dtype), (b, n_meta, d)), x_prompt], axis=1)
    x = jnp.pad(x, ((0, 0), (0, lp - l), (0, 0)))
    tables_p = _rope_tables(jnp.arange(lp))
    akv_p, aidx_p, bkv_p = [], [], []
    y_prompt = None
    for layer in range(depth):
        pr = _project(x, norm_g[layer], w_pad[layer], tables_p, TR)
        thr, cidx = _prompt_select(pr["iq"], pr["iw"], pr["ikb"], k_top_p)
        oa = _prompt_dsa(pr["iq"], pr["iw"], pr["ikb"], thr, cidx, pr["qa"], pr["kah"], pr["vah"])
        ob = _prompt_stick(pr["qb"], pr["kbh"], pr["vbh"])
        last = layer == depth - 1
        flat = lambda a: a.reshape(b * lp, a.shape[-1])
        xn = _merge(flat(x), flat(oa), flat(pr["ga"]), flat(ob), flat(pr["gb"]), flat(pr["ma"]), flat(pr["mb"]),
                    woa[layer], wob[layer], wout[layer], final_g, last, TR)
        x = xn.reshape(b, lp, d)
        akv_p.append(pr["kva"][:, :l].reshape(b, l, 2, A_KV_HEADS, HEAD_DIM))
        aidx_p.append(pr["ik"][:, :l])
        bkv_p.append(pr["kvb"][:, :l].reshape(b, l, 2, B_HEADS, HEAD_DIM))
    y_prompt = x[:, n_meta:l]

    db, ds, _ = x_sample.shape
    n_pages = page_table.shape[1]
    page = cache_a_idx.shape[2]
    past = n_pages * page
    k_top_s = min(TOPK, (past + ds) // 4)
    rows = db * ds
    rows_p = -(-rows // SUBLANES) * SUBLANES
    tr_s = rows_p if rows_p <= TR else TR
    rows_p = -(-rows_p // tr_s) * tr_s
    pos_s = past + (jnp.arange(rows_p) % ds)
    tables_s = _rope_tables(pos_s)
    xs = jnp.pad(x_sample.reshape(rows, d), ((0, rows_p - rows), (0, 0)))
    n_group = math.gcd(PAGE_GROUP, n_pages)
    assert page == LANES and (n_group * page) % SEL_COLS == 0
    idx_t = jnp.transpose(cache_a_idx, (0, 1, 3, 2))
    akv_t = jnp.transpose(cache_a_kv, (0, 1, 3, 4, 5, 2)).reshape(depth, -1, 2, A_KV_WIDTH, page)
    bkv_t = jnp.transpose(cache_b_kv, (0, 1, 3, 4, 5, 2)).reshape(depth, -1, 2, B_WIDTH, page)
    akv_s, aidx_s, bkv_s = [], [], []
    for layer in range(depth):
        pr = _project(xs[None], norm_g[layer], w_pad[layer], tables_s, tr_s)
        pr = {k: (v[:, :, :rows] if v.ndim == 4 else v[:, :rows]) for k, v in pr.items()}
        iq_rows = _sample_rows(pr["iq"], db, ds)
        iw = jnp.pad(pr["iw"][0].reshape(db, ds, IDX_HEADS), ((0, 0), (0, QROWS - ds), (0, 0)))
        wb_rows = jnp.broadcast_to(jnp.transpose(iw, (0, 2, 1)).reshape(db, IDX_HEADS * QROWS, 1),
                                   (db, IDX_HEADS * QROWS, page))
        ikt_new = _new_t(pr["ik"], db, ds, page, 1)[:, 0]
        scores = _sample_scores(page_table, iq_rows, wb_rows, idx_t, ikt_new, layer, n_group)
        thr, cidx = _sample_select(scores, k_top_s, past)
        qa_bd = _block_diag(_sample_rows(pr["qa"], db, ds), A_HEADS, A_GROUP, A_KV_HEADS)
        oa = _sample_dsa(page_table, scores, thr, cidx, qa_bd, akv_t, _new_t(pr["kva"], db, ds, page, 2),
                         past, layer, n_group)
        qb_bd = _block_diag(_sample_rows(pr["qb"], db, ds), B_HEADS, 1, B_HEADS)
        ob = _sample_stick(page_table, qb_bd, bkv_t, _new_t(pr["kvb"], db, ds, page, 2), layer, n_group)
        last = layer == depth - 1
        padr = lambda a: jnp.pad(a, ((0, rows_p - rows), (0, 0)))
        xs = _merge(xs, padr(_heads_to_tokens(oa, ds)), padr(pr["ga"][0]), padr(_heads_to_tokens(ob, ds)),
                    padr(pr["gb"][0]), padr(pr["ma"][0]), padr(pr["mb"][0]),
                    woa[layer], wob[layer], wout[layer], final_g, last, tr_s)
        akv_s.append(pr["kva"][0].reshape(db, ds, 2, A_KV_HEADS, HEAD_DIM))
        aidx_s.append(pr["ik"][0].reshape(db, ds, IDX_DIM))
        bkv_s.append(pr["kvb"][0].reshape(db, ds, 2, B_HEADS, HEAD_DIM))
    y_sample = xs[:rows].reshape(db, ds, d)

    return (y_prompt, y_sample, jnp.stack(akv_p), jnp.stack(aidx_p), jnp.stack(bkv_p),
            jnp.stack(akv_s), jnp.stack(aidx_s), jnp.stack(bkv_s))
```

```python
import functools
import math

import numpy as np
import jax
import jax.numpy as jnp
from jax import lax
from jax.experimental import pallas as pl
from jax.experimental.pallas import tpu as pltpu

F32 = jnp.float32
BF16 = jnp.bfloat16
I32 = jnp.int32

HEAD_DIM = 64
A_HEADS = 8
A_KV_HEADS = 4
A_GROUP = A_HEADS // A_KV_HEADS
IDX_HEADS = 8
IDX_DIM = 64
B_HEADS = 8
TOPK = 256
ROPE_THETA = 500000.0
EPS = 1e-6
A_WIDTH = A_HEADS * HEAD_DIM
A_KV_WIDTH = A_KV_HEADS * HEAD_DIM
B_WIDTH = B_HEADS * HEAD_DIM
IDX_WIDTH = IDX_HEADS * IDX_DIM

LANES = 128
SUBLANES = 8
VMEM_LIMIT = 56 * 1024 * 1024
INT_MIN = -(2 ** 31)
INT_MAX = 2 ** 31 - 1
NEG_BIG = -1e30
LOG2E = 1.4426950408889634
SP_CLAMP = 64.0

TQ = 256
TK = 256
TR = 256
SEL_ROWS = 128
SEL_COLS = TK
PAGE_GROUP = 8


def _nt_dot(a, b):
    return lax.dot_general(a, b, (((1,), (1,)), ((), ())), preferred_element_type=F32)


def _sigmoid(x):
    return 1.0 / (1.0 + jnp.exp(-x))


_OFF_QA = 0
_OFF_KA = _OFF_QA + A_WIDTH
_OFF_VA = _OFF_KA + A_KV_WIDTH
_OFF_IQ = _OFF_VA + A_KV_WIDTH
_OFF_IK = _OFF_IQ + IDX_WIDTH
_OFF_GA = _OFF_IK + LANES
_OFF_QB = _OFF_GA + A_WIDTH
_OFF_KB = _OFF_QB + B_WIDTH
_OFF_VB = _OFF_KB + B_WIDTH
_OFF_GB = _OFF_VB + B_WIDTH


def _pad_w_in(w, d_model):
    cut = _OFF_IK + IDX_DIM + IDX_HEADS
    pad = LANES - IDX_DIM - IDX_HEADS
    return jnp.concatenate([w[:, :cut], jnp.zeros((w.shape[0], pad), w.dtype), w[:, cut:]], axis=1)


def _rope_tables(pos):
    rot = HEAD_DIM // 4
    half = rot // 2
    inv = ROPE_THETA ** (-jnp.arange(half, dtype=F32) * (2.0 / rot))
    ang = pos.astype(F32)[:, None] * inv[None, :]
    cos = jnp.cos(ang)
    sin = jnp.sin(ang)
    n = pos.shape[0]
    one = jnp.ones((n, HEAD_DIM - rot), F32)
    zero = jnp.zeros((n, HEAD_DIM - rot), F32)
    zh = jnp.zeros((n, half), F32)
    c = jnp.concatenate([cos, cos, one], axis=1)
    sa = jnp.concatenate([-sin, zh, zero], axis=1)
    sb = jnp.concatenate([zh, sin, zero], axis=1)
    two = lambda t: jnp.concatenate([t, t], axis=1)
    return two(c), two(sa), two(sb)


def _rope(y, c, sa, sb):
    half = HEAD_DIM // 8
    return y * c + pltpu.roll(y, LANES - half, 1) * sa + pltpu.roll(y, half, 1) * sb


def _proj_kernel(x_ref, g_ref, w_ref, c_ref, sa_ref, sb_ref,
                 qa_ref, kva_ref, kah_ref, vah_ref, iq_ref, ik_ref, ikb_ref, iw_ref,
                 ga_ref, qb_ref, kvb_ref, kbh_ref, vbh_ref, gb_ref, ma_ref, mb_ref, *, d_model):
    x = x_ref[0]
    ms = jnp.mean(x * x, axis=-1, keepdims=True)
    h = (x * lax.rsqrt(ms + EPS) * g_ref[...]).astype(BF16)
    c = c_ref[...]
    sa = sa_ref[...]
    sb = sb_ref[...]

    def seg(off, width):
        return jnp.dot(h, w_ref[:, off:off + width], preferred_element_type=F32)

    def roped(u, groups):
        return [_rope(u[:, i * LANES:(i + 1) * LANES], c, sa, sb) for i in range(groups)]

    def put_heads(ref, groups, scale):
        for i, y in enumerate(groups):
            ys = (y * scale).astype(ref.dtype) if scale != 1.0 else y.astype(ref.dtype)
            ref[0, 2 * i] = ys[:, :HEAD_DIM]
            ref[0, 2 * i + 1] = ys[:, HEAD_DIM:]

    def split(u):
        return [u[:, i * LANES:(i + 1) * LANES] for i in range(u.shape[1] // LANES)]

    qscale = LOG2E * HEAD_DIM ** -0.5
    put_heads(qa_ref, roped(seg(_OFF_QA, A_WIDTH), A_WIDTH // LANES), qscale)
    ka = roped(seg(_OFF_KA, A_KV_WIDTH), A_KV_WIDTH // LANES)
    va = split(seg(_OFF_VA, A_KV_WIDTH))
    for i, y in enumerate(ka):
        kva_ref[0, :, i * LANES:(i + 1) * LANES] = y
    for i, y in enumerate(va):
        kva_ref[0, :, A_KV_WIDTH + i * LANES:A_KV_WIDTH + (i + 1) * LANES] = y
    put_heads(kah_ref, ka, 1.0)
    put_heads(vah_ref, va, 1.0)
    put_heads(iq_ref, roped(seg(_OFF_IQ, IDX_WIDTH), IDX_WIDTH // LANES), IDX_DIM ** -0.5)
    u = seg(_OFF_IK, LANES)
    ik = _rope(u, c, sa, sb)[:, :IDX_DIM]
    ik_ref[0] = ik
    ikb_ref[0] = ik.astype(BF16)
    iw_ref[0] = u[:, IDX_DIM:IDX_DIM + IDX_HEADS] * (IDX_HEADS ** -0.5)
    ga_ref[0] = seg(_OFF_GA, A_WIDTH)
    gb_ref[0] = seg(_OFF_GB, B_WIDTH)
    put_heads(qb_ref, split(seg(_OFF_QB, B_WIDTH)), qscale)
    kb = seg(_OFF_KB, B_WIDTH)
    vb = seg(_OFF_VB, B_WIDTH)
    kvb_ref[0, :, :B_WIDTH] = kb
    kvb_ref[0, :, B_WIDTH:] = vb
    put_heads(kbh_ref, split(kb), 1.0)
    put_heads(vbh_ref, split(vb), 1.0)
    off_ma = _OFF_GB + B_WIDTH
    ma_ref[0] = seg(off_ma, d_model)
    mb_ref[0] = seg(off_ma + d_model, d_model)


def _project(x, g, w_pad, tables, tr):
    G, L, D = x.shape
    c, sa, sb = tables
    wn = w_pad.shape[1]
    f = lambda w: jax.ShapeDtypeStruct((G, L, w), F32)
    hm = lambda nh: jax.ShapeDtypeStruct((G, nh, L, HEAD_DIM), BF16)
    out_shape = (
        hm(A_HEADS), f(2 * A_KV_WIDTH), hm(A_KV_HEADS), hm(A_KV_HEADS),
        hm(IDX_HEADS), f(IDX_DIM), jax.ShapeDtypeStruct((G, L, IDX_DIM), BF16), f(IDX_HEADS),
        f(A_WIDTH), hm(B_HEADS), f(2 * B_WIDTH), hm(B_HEADS), hm(B_HEADS), f(B_WIDTH), f(D), f(D),
    )
    row = lambda w: pl.BlockSpec((1, tr, w), lambda b, i: (b, i, 0))
    hms = lambda nh: pl.BlockSpec((1, nh, tr, HEAD_DIM), lambda b, i: (b, 0, i, 0))
    tab = pl.BlockSpec((tr, LANES), lambda b, i: (i, 0))
    out_specs = (
        hms(A_HEADS), row(2 * A_KV_WIDTH), hms(A_KV_HEADS), hms(A_KV_HEADS),
        hms(IDX_HEADS), row(IDX_DIM), row(IDX_DIM), row(IDX_HEADS),
        row(A_WIDTH), hms(B_HEADS), row(2 * B_WIDTH), hms(B_HEADS), hms(B_HEADS), row(B_WIDTH), row(D), row(D),
    )
    names = ("qa", "kva", "kah", "vah", "iq", "ik", "ikb", "iw",
             "ga", "qb", "kvb", "kbh", "vbh", "gb", "ma", "mb")
    outs = pl.pallas_call(
        functools.partial(_proj_kernel, d_model=D),
        grid=(G, L // tr),
        in_specs=[row(D), pl.BlockSpec((1, D), lambda b, i: (0, 0)),
                  pl.BlockSpec((D, wn), lambda b, i: (0, 0)), tab, tab, tab],
        out_specs=out_specs,
        out_shape=out_shape,
        compiler_params=pltpu.CompilerParams(
            dimension_semantics=("arbitrary", "arbitrary"), vmem_limit_bytes=VMEM_LIMIT),
        name="proj",
    )(x, g.reshape(1, D), w_pad, c, sa, sb)
    return dict(zip(names, outs))


def _score_keys(iq_ref, wb_ref, ik, visible):
    reps = ik.shape[0] // LANES
    s = None
    for h in range(IDX_HEADS):
        w = wb_ref[h]
        if reps > 1:
            w = jnp.concatenate([w] * reps, axis=1)
        t = w * jnp.maximum(_nt_dot(iq_ref[h], ik), 0.0)
        s = t if s is None else s + t
    return _keys_of(s, visible)


def _keys_of(s, visible):
    s = s + 0.0
    bits = pltpu.bitcast(s, I32)
    key = bits ^ ((bits >> 31) & INT_MAX)
    if visible is not None:
        key = jnp.where(visible, key, INT_MIN)
    return key


def _bcast_w(iw, wb_ref):
    for h in range(IDX_HEADS):
        wb_ref[h] = jnp.broadcast_to(iw[:, h:h + 1], (iw.shape[0], LANES))


def _select_threshold(load, npairs, k_top, tq, tk, max_cols):
    def count(pred):
        def body(jj, acc):
            for u in range(2):
                j = 2 * jj + u
                acc = acc + jnp.where(pred(j, load(j)), 1, 0)
            return acc

        acc = lax.fori_loop(0, npairs, body, jnp.zeros((tq, tk), I32))
        return jnp.sum(acc, axis=1, keepdims=True)

    def bit_body(it, prefix):
        bit = lax.shift_left(jnp.int32(1), jnp.int32(31) - it)
        cand = prefix + bit
        cb = jnp.broadcast_to(cand, (tq, tk))
        return jnp.where(count(lambda j, kj: kj >= cb) >= k_top, cand, prefix)

    thr = lax.fori_loop(0, 32, bit_body, jnp.full((tq, 1), INT_MIN, I32))
    tb = jnp.broadcast_to(thr, (tq, tk))
    n_gt = count(lambda j, kj: kj > tb)
    n_eq = count(lambda j, kj: kj == tb)
    need = k_top - n_gt
    few = thr == INT_MIN
    excess = jnp.logical_and(n_eq > need, jnp.logical_not(few))
    nbits = max(1, int(max_cols).bit_length())
    lane = lax.broadcasted_iota(I32, (tq, tk), 1)

    def tie_search(_):
        def bit_body2(it, lo):
            cand = lo + lax.shift_left(jnp.int32(1), jnp.int32(nbits - 1) - it)
            cb = jnp.broadcast_to(cand, (tq, tk))
            n = count(lambda j, kj: jnp.logical_and(kj == tb, j * tk + lane < cb))
            return jnp.where(n < need, cand, lo)

        return lax.fori_loop(0, nbits, bit_body2, jnp.zeros((tq, 1), I32))

    any_excess = jnp.max(jnp.where(excess, 1, 0)) > 0
    cidx = lax.cond(any_excess, tie_search, lambda _: jnp.zeros((tq, 1), I32), 0)
    cidx = jnp.where(excess, cidx, INT_MAX)
    cidx = jnp.where(few, -1, cidx)
    return thr, cidx


def _selected(key, col, thr_b, cidx_b):
    return jnp.logical_or(key > thr_b, jnp.logical_and(key == thr_b, col <= cidx_b))


def _prompt_select_kernel(iq_ref, iw_ref, ik_ref, thr_ref, cidx_ref, keys_ref, wb_ref, *, k_top):
    i = pl.program_id(1)
    _bcast_w(iw_ref[0], wb_ref)
    row = i * TQ + lax.broadcasted_iota(I32, (TQ, TK), 0)
    lane = lax.broadcasted_iota(I32, (TQ, TK), 1)

    def fill(j, _):
        ik = ik_ref[0, pl.ds(pl.multiple_of(j * TK, TK), TK), :]
        keys_ref[j] = _score_keys(iq_ref.at[0], wb_ref, ik, (j * TK + lane) <= row)
        return 0

    nblk = i + 1
    lax.fori_loop(0, nblk, fill, 0)
    keys_ref[nblk] = jnp.full((TQ, TK), INT_MIN, I32)
    npairs = (nblk + 1) // 2

    def chunk(c, _):
        r0 = pl.multiple_of(c * SEL_ROWS, SEL_ROWS)
        load = lambda j: keys_ref[j, pl.ds(r0, SEL_ROWS), :]
        thr, cidx = _select_threshold(load, npairs, k_top, SEL_ROWS, TK, keys_ref.shape[0] * TK)
        thr_ref[0, pl.ds(r0, SEL_ROWS), :] = thr
        cidx_ref[0, pl.ds(r0, SEL_ROWS), :] = cidx
        return 0

    lax.fori_loop(0, TQ // SEL_ROWS, chunk, 0)


def _prompt_select(iq, iw, ikb, k_top):
    B, H, L, _ = iq.shape
    nq = L // TQ
    assert TQ == TK and TQ % SEL_ROWS == 0
    col1 = pl.BlockSpec((1, TQ, 1), lambda b, i: (b, i, 0))
    return pl.pallas_call(
        functools.partial(_prompt_select_kernel, k_top=k_top),
        grid=(B, nq),
        in_specs=[pl.BlockSpec((1, H, TQ, IDX_DIM), lambda b, i: (b, 0, i, 0)),
                  pl.BlockSpec((1, TQ, IDX_HEADS), lambda b, i: (b, i, 0)),
                  pl.BlockSpec((1, L, IDX_DIM), lambda b, i: (b, 0, 0))],
        out_specs=(col1, col1),
        out_shape=(jax.ShapeDtypeStruct((B, L, 1), I32), jax.ShapeDtypeStruct((B, L, 1), I32)),
        scratch_shapes=[pltpu.VMEM((L // TK + 1, TQ, TK), I32), pltpu.VMEM((IDX_HEADS, TQ, LANES), F32)],
        compiler_params=pltpu.CompilerParams(
            dimension_semantics=("arbitrary", "arbitrary"), vmem_limit_bytes=VMEM_LIMIT),
        name="prompt_select",
    )(iq, iw, ikb)


def _tri_tables(nq, descending):
    it, jt = [], []
    for i in range(nq):
        js = range(i, -1, -1) if descending else range(i + 1)
        for j in js:
            it.append(i)
            jt.append(j)
    return jnp.asarray(np.array(it, np.int32)), jnp.asarray(np.array(jt, np.int32))


def _prompt_dsa_kernel(it_ref, jt_ref, iq_ref, iw_ref, ik_ref, thr_ref, cidx_ref, q_ref, k_ref, v_ref,
                       o_ref, wb_ref, m_ref, l_ref, acc_ref):
    p = pl.program_id(1)
    i = it_ref[p]
    j = jt_ref[p]

    @pl.when(j == 0)
    def _():
        _bcast_w(iw_ref[0], wb_ref)
        m_ref[...] = jnp.full(m_ref.shape, NEG_BIG, F32)
        l_ref[...] = jnp.zeros(l_ref.shape, F32)
        acc_ref[...] = jnp.zeros(acc_ref.shape, F32)

    row = i * TQ + lax.broadcasted_iota(I32, (TQ, TK), 0)
    col = j * TK + lax.broadcasted_iota(I32, (TQ, TK), 1)
    key = _score_keys(iq_ref.at[0], wb_ref, ik_ref[0], col <= row)
    sel = _selected(key, col, jnp.broadcast_to(thr_ref[0], (TQ, TK)), jnp.broadcast_to(cidx_ref[0], (TQ, TK)))
    reps = TK // LANES

    for h in range(A_HEADS):
        g = h // A_GROUP
        s = _nt_dot(q_ref[0, h], k_ref[0, g])
        s = jnp.where(sel, s, 2.0 * NEG_BIG)
        m_prev = m_ref[h]
        m_new = jnp.maximum(m_prev, jnp.max(s, axis=1, keepdims=True))
        alpha = jnp.exp2(m_prev - m_new)
        pr = jnp.exp2(s - jnp.concatenate([m_new] * reps, axis=1))
        l_ref[h] = alpha * l_ref[h] + jnp.sum(pr, axis=1, keepdims=True)
        acc_ref[h] = alpha[:, :HEAD_DIM] * acc_ref[h] + jnp.dot(
            pr.astype(BF16), v_ref[0, g], preferred_element_type=F32)
        m_ref[h] = m_new

    @pl.when(j == i)
    def _():
        for h in range(A_HEADS):
            o_ref[0, :, h * HEAD_DIM:(h + 1) * HEAD_DIM] = acc_ref[h] / l_ref[h][:, :HEAD_DIM]


def _prompt_dsa(iq, iw, ikb, thr, cidx, qa, kah, vah):
    B, H, L, _ = qa.shape
    nq = L // TQ
    it, jt = _tri_tables(nq, descending=False)
    qspec = lambda nh: pl.BlockSpec((1, nh, TQ, HEAD_DIM), lambda b, p, it, jt: (b, 0, it[p], 0))
    kspec = lambda nh: pl.BlockSpec((1, nh, TK, HEAD_DIM), lambda b, p, it, jt: (b, 0, jt[p], 0))
    qrow = lambda w: pl.BlockSpec((1, TQ, w), lambda b, p, it, jt: (b, it[p], 0))
    grid_spec = pltpu.PrefetchScalarGridSpec(
        num_scalar_prefetch=2,
        grid=(B, int(it.shape[0])),
        in_specs=[qspec(IDX_HEADS), qrow(IDX_HEADS),
                  pl.BlockSpec((1, TK, IDX_DIM), lambda b, p, it, jt: (b, jt[p], 0)),
                  qrow(1), qrow(1), qspec(A_HEADS), kspec(A_KV_HEADS), kspec(A_KV_HEADS)],
        out_specs=qrow(A_WIDTH),
        scratch_shapes=[pltpu.VMEM((IDX_HEADS, TQ, LANES), F32),
                        pltpu.VMEM((A_HEADS, TQ, LANES), F32),
                        pltpu.VMEM((A_HEADS, TQ, LANES), F32),
                        pltpu.VMEM((A_HEADS, TQ, HEAD_DIM), F32)],
    )
    return pl.pallas_call(
        _prompt_dsa_kernel,
        grid_spec=grid_spec,
        out_shape=jax.ShapeDtypeStruct((B, L, A_WIDTH), F32),
        compiler_params=pltpu.CompilerParams(
            dimension_semantics=("arbitrary", "arbitrary"), vmem_limit_bytes=VMEM_LIMIT),
        name="prompt_dsa",
    )(it, jt, iq, iw, ikb, thr, cidx, qa, kah, vah)


def _suffix_matrix(n):
    r = lax.broadcasted_iota(I32, (2 * n, n), 0)
    r = jnp.where(r >= n, r - n, r)
    c = lax.broadcasted_iota(I32, (2 * n, n), 1)
    return jnp.where(r >= c, 1.0, 0.0).astype(BF16)


def _stick_tile(z, earlier, r_b, suf):
    incl = jnp.dot(_stick_split(z, earlier), suf, preferred_element_type=F32)
    return _stick_weights(z, incl, earlier, r_b)


def _stick_split(z, earlier):
    p = jnp.exp2(jnp.minimum(z, SP_CLAMP))
    sp = jnp.maximum(z, jnp.log(1.0 + p) * LOG2E)
    if earlier is not None:
        sp = jnp.where(earlier, sp, 0.0)
    hi = sp.astype(BF16)
    lo = (sp - hi.astype(F32)).astype(BF16)
    return jnp.concatenate([hi, lo], axis=1)


def _stick_weights(z, incl, earlier, r_b):
    reps = z.shape[1] // LANES
    rb = jnp.concatenate([r_b] * reps, axis=1) if reps > 1 else r_b
    a = jnp.exp2((z - incl) - rb)
    if earlier is not None:
        a = jnp.where(earlier, a, 0.0)
    total = jnp.broadcast_to(incl[:, 0:1], r_b.shape)
    return a, total


def _prompt_stick_kernel(it_ref, jt_ref, q_ref, k_ref, v_ref, o_ref, r_ref, acc_ref, suf_ref):
    p = pl.program_id(1)
    i = it_ref[p]
    j = jt_ref[p]

    @pl.when(j == i)
    def _():
        r_ref[...] = jnp.zeros(r_ref.shape, F32)
        acc_ref[...] = jnp.zeros(acc_ref.shape, F32)
        suf_ref[...] = _suffix_matrix(TK)

    def heads(diagonal):
        earlier = None
        if diagonal:
            earlier = lax.broadcasted_iota(I32, (TQ, TK), 1) < lax.broadcasted_iota(I32, (TQ, TK), 0)
        zs = [_nt_dot(q_ref[0, h], k_ref[0, h]) for h in range(B_HEADS)]
        split = jnp.concatenate([_stick_split(z, earlier) for z in zs], axis=0)
        incl = jnp.dot(split, suf_ref[...], preferred_element_type=F32)
        for h in range(B_HEADS):
            a, total = _stick_weights(zs[h], incl[h * TQ:(h + 1) * TQ], earlier, r_ref[h])
            acc_ref[h] = acc_ref[h] + jnp.dot(a.astype(BF16), v_ref[0, h], preferred_element_type=F32)
            r_ref[h] = r_ref[h] + total

    @pl.when(j == i)
    def _():
        heads(True)

    @pl.when(j < i)
    def _():
        heads(False)

    @pl.when(j == 0)
    def _():
        for h in range(B_HEADS):
            o_ref[0, :, h * HEAD_DIM:(h + 1) * HEAD_DIM] = acc_ref[h]


def _prompt_stick(qb, kbh, vbh):
    B, H, L, _ = qb.shape
    nq = L // TQ
    it, jt = _tri_tables(nq, descending=True)
    grid_spec = pltpu.PrefetchScalarGridSpec(
        num_scalar_prefetch=2,
        grid=(B, int(it.shape[0])),
        in_specs=[pl.BlockSpec((1, H, TQ, HEAD_DIM), lambda b, p, it, jt: (b, 0, it[p], 0)),
                  pl.BlockSpec((1, H, TK, HEAD_DIM), lambda b, p, it, jt: (b, 0, jt[p], 0)),
                  pl.BlockSpec((1, H, TK, HEAD_DIM), lambda b, p, it, jt: (b, 0, jt[p], 0))],
        out_specs=pl.BlockSpec((1, TQ, B_WIDTH), lambda b, p, it, jt: (b, it[p], 0)),
        scratch_shapes=[pltpu.VMEM((B_HEADS, TQ, LANES), F32),
                        pltpu.VMEM((B_HEADS, TQ, HEAD_DIM), F32),
                        pltpu.VMEM((2 * TK, TK), BF16)],
    )
    return pl.pallas_call(
        _prompt_stick_kernel,
        grid_spec=grid_spec,
        out_shape=jax.ShapeDtypeStruct((B, L, B_WIDTH), F32),
        compiler_params=pltpu.CompilerParams(
            dimension_semantics=("arbitrary", "arbitrary"), vmem_limit_bytes=VMEM_LIMIT),
        name="prompt_stick",
    )(it, jt, qb, kbh, vbh)


def _merge_kernel(x_ref, oa_ref, ga_ref, ob_ref, gb_ref, ma_ref, mb_ref, woa_ref, wob_ref, wout_ref, fg_ref,
                  o_ref, *, final_norm):
    ga = ga_ref[...]
    gb = gb_ref[...]
    ha = (oa_ref[...] * (ga * _sigmoid(ga))).astype(BF16)
    hb = (ob_ref[...] * (gb * _sigmoid(gb))).astype(BF16)
    pa = jnp.dot(ha, woa_ref[...], preferred_element_type=F32)
    pb = jnp.dot(hb, wob_ref[...], preferred_element_type=F32)
    mix = (_sigmoid(ma_ref[...]) * pa + _sigmoid(mb_ref[...]) * pb).astype(BF16)
    x = x_ref[...] + jnp.dot(mix, wout_ref[...], preferred_element_type=F32)
    if final_norm:
        ms = jnp.mean(x * x, axis=-1, keepdims=True)
        x = x * lax.rsqrt(ms + EPS) * fg_ref[...]
    o_ref[...] = x


def _merge(x, oa, ga, ob, gb, ma, mb, woa, wob, wout, final_g, final_norm, tr):
    R, D = x.shape
    row = lambda w: pl.BlockSpec((tr, w), lambda i: (i, 0))
    full = lambda a: pl.BlockSpec(a.shape, lambda i: (0, 0))
    fg = final_g.reshape(1, D)
    return pl.pallas_call(
        functools.partial(_merge_kernel, final_norm=final_norm),
        grid=(R // tr,),
        in_specs=[row(D), row(A_WIDTH), row(A_WIDTH), row(B_WIDTH), row(B_WIDTH), row(D), row(D),
                  full(woa), full(wob), full(wout), full(fg)],
        out_specs=row(D),
        out_shape=jax.ShapeDtypeStruct((R, D), F32),
        compiler_params=pltpu.CompilerParams(
            dimension_semantics=("arbitrary",), vmem_limit_bytes=VMEM_LIMIT),
        name="merge",
    )(x, oa, ga, ob, gb, ma, mb, woa, wob, wout, fg)


QROWS = SUBLANES


def _page_specs(n_group, n_pages, block, layer, reverse):
    n_steps = n_pages // n_group
    zeros = (0,) * (len(block) - 2)

    def spec(g):
        if reverse:
            idx = lambda b, j, pt: (layer, pt[b, n_pages - jnp.maximum(j, 1) * n_group + g]) + zeros
        else:
            idx = lambda b, j, pt: (layer, pt[b, jnp.minimum(j, n_steps - 1) * n_group + g]) + zeros
        return pl.BlockSpec(block, idx)

    return [spec(g) for g in range(n_group)]


def _sample_score_kernel(pt_ref, iq_ref, wb_ref, *refs, n_group, n_steps):
    page_refs, new_ref, s_ref = refs[:n_group], refs[n_group], refs[n_group + 1]
    j = pl.program_id(1)
    P = new_ref.shape[2]

    def scores(ikt):
        d = jnp.dot(iq_ref[0], ikt, preferred_element_type=F32)
        t = wb_ref[0] * jnp.maximum(d, 0.0)
        s = t[0:QROWS]
        for h in range(1, IDX_HEADS):
            s = s + t[h * QROWS:(h + 1) * QROWS]
        return s

    def put(g, s):
        per = SEL_COLS // P
        s_ref[g // per, :, (g % per) * P:(g % per + 1) * P] = s

    @pl.when(j < n_steps)
    def _():
        for g in range(n_group):
            put(g, scores(page_refs[g][0, 0].astype(BF16)))

    @pl.when(j == n_steps)
    def _():
        put(0, scores(new_ref[0]))
        for g in range(1, n_group):
            put(g, jnp.zeros((QROWS, P), F32))


def _sample_scores(page_table, iq_rows, wb_rows, cache_idx_t, ikt_new, layer, n_group):
    DB, n_pages = page_table.shape
    P = cache_idx_t.shape[3]
    R = iq_rows.shape[1]
    n_steps = n_pages // n_group
    blk = n_group * P // SEL_COLS
    grid_spec = pltpu.PrefetchScalarGridSpec(
        num_scalar_prefetch=1,
        grid=(DB, n_steps + 1),
        in_specs=[pl.BlockSpec((1, R, IDX_DIM), lambda b, j, pt: (b, 0, 0)),
                  pl.BlockSpec((1, R, P), lambda b, j, pt: (b, 0, 0))]
                 + _page_specs(n_group, n_pages, (1, 1, IDX_DIM, P), layer, False)
                 + [pl.BlockSpec((1, IDX_DIM, P), lambda b, j, pt: (b, 0, 0))],
        out_specs=pl.BlockSpec((blk, QROWS, SEL_COLS), lambda b, j, pt: (j, b, 0)),
    )
    return pl.pallas_call(
        functools.partial(_sample_score_kernel, n_group=n_group, n_steps=n_steps),
        grid_spec=grid_spec,
        out_shape=jax.ShapeDtypeStruct(((n_steps + 1) * blk, DB * QROWS, SEL_COLS), F32),
        compiler_params=pltpu.CompilerParams(
            dimension_semantics=("arbitrary", "arbitrary"), vmem_limit_bytes=VMEM_LIMIT),
        name="sample_scores",
    )(page_table, iq_rows, wb_rows, *([cache_idx_t] * n_group), ikt_new)


def _sample_visible(j, rows, past):
    lane = lax.broadcasted_iota(I32, (rows, SEL_COLS), 1)
    t = lax.broadcasted_iota(I32, (rows, SEL_COLS), 0) % QROWS
    return j * SEL_COLS + lane <= past + t


def _sample_select_kernel(s_ref, thr_ref, cidx_ref, keys_ref, *, k_top, past):
    nblk = s_ref.shape[0]

    def fill(j, _):
        keys_ref[j] = _keys_of(s_ref[j], _sample_visible(j, SEL_ROWS, past))
        return 0

    lax.fori_loop(0, nblk, fill, 0)
    if nblk % 2:
        keys_ref[nblk] = jnp.full((SEL_ROWS, SEL_COLS), INT_MIN, I32)
    load = lambda j: keys_ref[j]
    thr, cidx = _select_threshold(load, (nblk + 1) // 2, k_top, SEL_ROWS, SEL_COLS, keys_ref.shape[0] * SEL_COLS)
    thr_ref[...] = thr
    cidx_ref[...] = cidx


def _sample_select(scores, k_top, past):
    nblk, rows, _ = scores.shape
    assert rows % SEL_ROWS == 0
    col1 = pl.BlockSpec((SEL_ROWS, 1), lambda c: (c, 0))
    return pl.pallas_call(
        functools.partial(_sample_select_kernel, k_top=k_top, past=past),
        grid=(rows // SEL_ROWS,),
        in_specs=[pl.BlockSpec((nblk, SEL_ROWS, SEL_COLS), lambda c: (0, c, 0))],
        out_specs=(col1, col1),
        out_shape=(jax.ShapeDtypeStruct((rows, 1), I32), jax.ShapeDtypeStruct((rows, 1), I32)),
        scratch_shapes=[pltpu.VMEM((nblk + nblk % 2, SEL_ROWS, SEL_COLS), I32)],
        compiler_params=pltpu.CompilerParams(
            dimension_semantics=("arbitrary",), vmem_limit_bytes=VMEM_LIMIT),
        name="sample_select",
    )(scores)


def _sample_dsa_kernel(pt_ref, s_ref, thr_ref, cidx_ref, q_ref, *refs, n_group, n_steps, past):
    page_refs, new_ref, o_ref, m_ref, l_ref, acc_ref = refs[:n_group], *refs[n_group:]
    j = pl.program_id(1)
    P = new_ref.shape[3]
    nblk = s_ref.shape[0]
    wide = A_KV_WIDTH // LANES

    @pl.when(j == 0)
    def _():
        m_ref[...] = jnp.full(m_ref.shape, NEG_BIG, F32)
        l_ref[...] = jnp.zeros(l_ref.shape, F32)
        acc_ref[...] = jnp.zeros(acc_ref.shape, F32)

    def step(kts, vts):
        thr_b = jnp.broadcast_to(thr_ref[...], (QROWS, SEL_COLS))
        cidx_b = jnp.broadcast_to(cidx_ref[...], (QROWS, SEL_COLS))
        lane = lax.broadcasted_iota(I32, (QROWS, SEL_COLS), 1)
        sels = []
        for u in range(nblk):
            jb = j * nblk + u
            key = _keys_of(s_ref[u], _sample_visible(jb, QROWS, past))
            sels.append(_selected(key, jb * SEL_COLS + lane, thr_b, cidx_b).astype(I32))
        sel8 = jnp.concatenate(sels, axis=1)
        sel = jnp.concatenate([sel8] * A_HEADS, axis=0) > 0
        lg = jnp.concatenate([jnp.dot(q_ref[0], kt, preferred_element_type=F32) for kt in kts], axis=1)
        lg = jnp.where(sel, lg, 2.0 * NEG_BIG)
        m_prev = m_ref[...]
        m_new = jnp.maximum(m_prev, jnp.max(lg, axis=1, keepdims=True))
        alpha = jnp.exp2(m_prev - m_new)
        pr = jnp.exp2(lg - jnp.concatenate([m_new] * len(kts), axis=1))
        l_ref[...] = alpha * l_ref[...] + jnp.sum(pr, axis=1, keepdims=True)
        pv = None
        for g, vt in enumerate(vts):
            t = _nt_dot(pr[:, g * P:(g + 1) * P].astype(BF16), vt)
            pv = t if pv is None else pv + t
        acc_ref[...] = jnp.concatenate([alpha] * wide, axis=1) * acc_ref[...] + pv
        m_ref[...] = m_new

    @pl.when(j < n_steps)
    def _():
        step([r[0, 0, 0].astype(BF16) for r in page_refs], [r[0, 0, 1].astype(BF16) for r in page_refs])

    @pl.when(j == n_steps)
    def _():
        zero = jnp.zeros((A_KV_WIDTH, P), BF16)
        step([new_ref[0, 0]] + [zero] * (n_group - 1), [new_ref[0, 1]] + [zero] * (n_group - 1))
        out = acc_ref[...] / jnp.concatenate([l_ref[...]] * wide, axis=1)
        for h in range(A_HEADS):
            g = h // A_GROUP
            o_ref[0, h] = out[h * QROWS:(h + 1) * QROWS, g * HEAD_DIM:(g + 1) * HEAD_DIM]


def _sample_dsa(page_table, scores, thr, cidx, q_bd, cache_kv_t, kvt_new, past, layer, n_group):
    DB, n_pages = page_table.shape
    P = cache_kv_t.shape[4]
    R = q_bd.shape[1]
    n_steps = n_pages // n_group
    blk = n_group * P // SEL_COLS
    col1 = pl.BlockSpec((QROWS, 1), lambda b, j, pt: (b, 0))
    grid_spec = pltpu.PrefetchScalarGridSpec(
        num_scalar_prefetch=1,
        grid=(DB, n_steps + 1),
        in_specs=[pl.BlockSpec((blk, QROWS, SEL_COLS), lambda b, j, pt: (j, b, 0)), col1, col1,
                  pl.BlockSpec((1, R, A_KV_WIDTH), lambda b, j, pt: (b, 0, 0))]
                 + _page_specs(n_group, n_pages, (1, 1, 2, A_KV_WIDTH, P), layer, False)
                 + [pl.BlockSpec((1, 2, A_KV_WIDTH, P), lambda b, j, pt: (b, 0, 0, 0))],
        out_specs=pl.BlockSpec((1, A_HEADS, QROWS, HEAD_DIM), lambda b, j, pt: (b, 0, 0, 0)),
        scratch_shapes=[pltpu.VMEM((R, LANES), F32), pltpu.VMEM((R, LANES), F32),
                        pltpu.VMEM((R, A_KV_WIDTH), F32)],
    )
    return pl.pallas_call(
        functools.partial(_sample_dsa_kernel, n_group=n_group, n_steps=n_steps, past=past),
        grid_spec=grid_spec,
        out_shape=jax.ShapeDtypeStruct((DB, A_HEADS, QROWS, HEAD_DIM), F32),
        compiler_params=pltpu.CompilerParams(
            dimension_semantics=("arbitrary", "arbitrary"), vmem_limit_bytes=VMEM_LIMIT),
        name="sample_dsa",
    )(page_table, scores, thr, cidx, q_bd, *([cache_kv_t] * n_group), kvt_new)


def _sample_stick_kernel(pt_ref, q_ref, *refs, n_group, n_steps):
    page_refs, new_ref, o_ref, r_ref, acc_ref, suf_ref = refs[:n_group], *refs[n_group:]
    j = pl.program_id(1)
    P = new_ref.shape[3]

    @pl.when(j == 0)
    def _():
        r_ref[...] = jnp.zeros(r_ref.shape, F32)
        acc_ref[...] = jnp.zeros(acc_ref.shape, F32)
        suf_ref[...] = _suffix_matrix(P)

    def page(kt, vt, earlier):
        z = jnp.dot(q_ref[0], kt, preferred_element_type=F32)
        a, total = _stick_tile(z, earlier, r_ref[...], suf_ref[...])
        acc_ref[...] = acc_ref[...] + _nt_dot(a.astype(BF16), vt)
        r_ref[...] = r_ref[...] + total

    @pl.when(j == 0)
    def _():
        R = q_ref.shape[1]
        t = lax.broadcasted_iota(I32, (R, P), 0) % QROWS
        s = lax.broadcasted_iota(I32, (R, P), 1)
        page(new_ref[0, 0], new_ref[0, 1], s < t)

    @pl.when(j > 0)
    def _():
        for g in range(n_group - 1, -1, -1):
            page(page_refs[g][0, 0, 0].astype(BF16), page_refs[g][0, 0, 1].astype(BF16), None)

    @pl.when(j == n_steps)
    def _():
        acc = acc_ref[...]
        for h in range(B_HEADS):
            o_ref[0, h] = acc[h * QROWS:(h + 1) * QROWS, h * HEAD_DIM:(h + 1) * HEAD_DIM]


def _sample_stick(page_table, q_bd, cache_kv_t, kvt_new, layer, n_group):
    DB, n_pages = page_table.shape
    P = cache_kv_t.shape[4]
    R = q_bd.shape[1]
    n_steps = n_pages // n_group
    grid_spec = pltpu.PrefetchScalarGridSpec(
        num_scalar_prefetch=1,
        grid=(DB, n_steps + 1),
        in_specs=[pl.BlockSpec((1, R, B_WIDTH), lambda b, j, pt: (b, 0, 0))]
                 + _page_specs(n_group, n_pages, (1, 1, 2, B_WIDTH, P), layer, True)
                 + [pl.BlockSpec((1, 2, B_WIDTH, P), lambda b, j, pt: (b, 0, 0, 0))],
        out_specs=pl.BlockSpec((1, B_HEADS, QROWS, HEAD_DIM), lambda b, j, pt: (b, 0, 0, 0)),
        scratch_shapes=[pltpu.VMEM((R, LANES), F32), pltpu.VMEM((R, B_WIDTH), F32),
                        pltpu.VMEM((2 * P, P), BF16)],
    )
    return pl.pallas_call(
        functools.partial(_sample_stick_kernel, n_group=n_group, n_steps=n_steps),
        grid_spec=grid_spec,
        out_shape=jax.ShapeDtypeStruct((DB, B_HEADS, QROWS, HEAD_DIM), F32),
        compiler_params=pltpu.CompilerParams(
            dimension_semantics=("arbitrary", "arbitrary"), vmem_limit_bytes=VMEM_LIMIT),
        name="sample_stick",
    )(page_table, q_bd, *([cache_kv_t] * n_group), kvt_new)


def _sample_rows(hm, db, ds):
    nh = hm.shape[1]
    a = hm[0].reshape(nh, db, ds, HEAD_DIM)
    a = jnp.pad(a, ((0, 0), (0, 0), (0, QROWS - ds), (0, 0)))
    return jnp.transpose(a, (1, 0, 2, 3)).reshape(db, nh * QROWS, HEAD_DIM)


def _block_diag(rows, nh, group, width_heads):
    db = rows.shape[0]
    r = rows.reshape(db, nh, QROWS, 1, HEAD_DIM)
    onehot = (jnp.arange(nh)[:, None] // group == jnp.arange(width_heads)[None, :])
    out = jnp.where(onehot[None, :, None, :, None], r, jnp.zeros((), rows.dtype))
    return out.reshape(db, nh * QROWS, width_heads * HEAD_DIM)


def _new_t(tok, db, ds, page, parts):
    w = tok.shape[-1]
    a = jnp.transpose(tok.reshape(db, ds, parts, w // parts), (0, 2, 3, 1))
    return jnp.pad(a, ((0, 0), (0, 0), (0, 0), (0, page - ds))).astype(BF16)


def _heads_to_tokens(o, ds):
    db, nh = o.shape[0], o.shape[1]
    return jnp.transpose(o[:, :, :ds], (0, 2, 1, 3)).reshape(db * ds, nh * HEAD_DIM)


def kernel(x_prompt, x_sample, cache_a_kv, cache_a_idx, cache_b_kv, page_table,
           meta, norm_g, w_in, w_oa, w_ob, w_out, final_g):
    depth = norm_g.shape[0]
    b, seq, d = x_prompt.shape
    n_meta = meta.shape[0]
    l = n_meta + seq
    blk = math.lcm(TQ, TK, TR)
    lp = -(-l // blk) * blk
    k_top_p = min(TOPK, seq // 4)

    w_pad = [_pad_w_in(w_in[i], d).astype(BF16) for i in range(depth)]
    woa = w_oa.astype(BF16)
    wob = w_ob.astype(BF16)
    wout = w_out.astype(BF16)

    x = jnp.concatenate([jnp.broadcast_to(meta[None].astype(x_prompt.dtype), (b, n_meta, d)), x_prompt], axis=1)
    x = jnp.pad(x, ((0, 0), (0, lp - l), (0, 0)))
    tables_p = _rope_tables(jnp.arange(lp))
    akv_p, aidx_p, bkv_p = [], [], []
    y_prompt = None
    for layer in range(depth):
        pr = _project(x, norm_g[layer], w_pad[layer], tables_p, TR)
        thr, cidx = _prompt_select(pr["iq"], pr["iw"], pr["ikb"], k_top_p)
        oa = _prompt_dsa(pr["iq"], pr["iw"], pr["ikb"], thr, cidx, pr["qa"], pr["kah"], pr["vah"])
        ob = _prompt_stick(pr["qb"], pr["kbh"], pr["vbh"])
        last = layer == depth - 1
        flat = lambda a: a.reshape(b * lp, a.shape[-1])
        xn = _merge(flat(x), flat(oa), flat(pr["ga"]), flat(ob), flat(pr["gb"]), flat(pr["ma"]), flat(pr["mb"]),
                    woa[layer], wob[layer], wout[layer], final_g, last, TR)
        x = xn.reshape(b, lp, d)
        akv_p.append(pr["kva"][:, :l].reshape(b, l, 2, A_KV_HEADS, HEAD_DIM))
        aidx_p.append(pr["ik"][:, :l])
        bkv_p.append(pr["kvb"][:, :l].reshape(b, l, 2, B_HEADS, HEAD_DIM))
    y_prompt = x[:, n_meta:l]

    db, ds, _ = x_sample.shape
    n_pages = page_table.shape[1]
    page = cache_a_idx.shape[2]
    past = n_pages * page
    k_top_s = min(TOPK, (past + ds) // 4)
    rows = db * ds
    rows_p = -(-rows // SUBLANES) * SUBLANES
    tr_s = rows_p if rows_p <= TR else TR
    rows_p = -(-rows_p // tr_s) * tr_s
    pos_s = past + (jnp.arange(rows_p) % ds)
    tables_s = _rope_tables(pos_s)
    xs = jnp.pad(x_sample.reshape(rows, d), ((0, rows_p - rows), (0, 0)))
    n_group = math.gcd(PAGE_GROUP, n_pages)
    assert page == LANES and (n_group * page) % SEL_COLS == 0
    idx_t = jnp.transpose(cache_a_idx, (0, 1, 3, 2))
    akv_t = jnp.transpose(cache_a_kv, (0, 1, 3, 4, 5, 2)).reshape(depth, -1, 2, A_KV_WIDTH, page)
    bkv_t = jnp.transpose(cache_b_kv, (0, 1, 3, 4, 5, 2)).reshape(depth, -1, 2, B_WIDTH, page)
    akv_s, aidx_s, bkv_s = [], [], []
    for layer in range(depth):
        pr = _project(xs[None], norm_g[layer], w_pad[layer], tables_s, tr_s)
        pr = {k: (v[:, :, :rows] if v.ndim == 4 else v[:, :rows]) for k, v in pr.items()}
        iq_rows = _sample_rows(pr["iq"], db, ds)
        iw = jnp.pad(pr["iw"][0].reshape(db, ds, IDX_HEADS), ((0, 0), (0, QROWS - ds), (0, 0)))
        wb_rows = jnp.broadcast_to(jnp.transpose(iw, (0, 2, 1)).reshape(db, IDX_HEADS * QROWS, 1),
                                   (db, IDX_HEADS * QROWS, page))
        ikt_new = _new_t(pr["ik"], db, ds, page, 1)[:, 0]
        scores = _sample_scores(page_table, iq_rows, wb_rows, idx_t, ikt_new, layer, n_group)
        thr, cidx = _sample_select(scores, k_top_s, past)
        qa_bd = _block_diag(_sample_rows(pr["qa"], db, ds), A_HEADS, A_GROUP, A_KV_HEADS)
        oa = _sample_dsa(page_table, scores, thr, cidx, qa_bd, akv_t, _new_t(pr["kva"], db, ds, page, 2),
                         past, layer, n_group)
        qb_bd = _block_diag(_sample_rows(pr["qb"], db, ds), B_HEADS, 1, B_HEADS)
        ob = _sample_stick(page_table, qb_bd, bkv_t, _new_t(pr["kvb"], db, ds, page, 2), layer, n_group)
        last = layer == depth - 1
        padr = lambda a: jnp.pad(a, ((0, rows_p - rows), (0, 0)))
        xs = _merge(xs, padr(_heads_to_tokens(oa, ds)), padr(pr["ga"][0]), padr(_heads_to_tokens(ob, ds)),
                    padr(pr["gb"][0]), padr(pr["ma"][0]), padr(pr["mb"][0]),
                    woa[layer], wob[layer], wout[layer], final_g, last, tr_s)
        akv_s.append(pr["kva"][0].reshape(db, ds, 2, A_KV_HEADS, HEAD_DIM))
        aidx_s.append(pr["ik"][0].reshape(db, ds, IDX_DIM))
        bkv_s.append(pr["kvb"][0].reshape(db, ds, 2, B_HEADS, HEAD_DIM))
    y_sample = xs[:rows].reshape(db, ds, d)

    return (y_prompt, y_sample, jnp.stack(akv_p), jnp.stack(aidx_p), jnp.stack(bkv_p),
            jnp.stack(akv_s), jnp.stack(aidx_s), jnp.stack(bkv_s))
```
